```python
import math
import jax, jax.numpy as jnp
from jax import lax
import numpy as np

D_MODEL = 1024
BATCH = 32
SEQ = 2048
DEPTH = 2

MIX_WIDTH = D_MODEL
ATTN_HEADS = 8
HEAD_DIM = 64
ATTN_WIDTH = ATTN_HEADS * HEAD_DIM
DILATED_PATTERNS = ((128, 1), (512, 4), (2048, 16))
QUERY_BLOCK = 128
ROPE_THETA = 500000.0
ROPE_DIM = HEAD_DIM // 4
POOL_WINDOWS = (2, 4, 8, 16)
POOL_GROUPS = len(POOL_WINDOWS)
POOL_WIDTH = MIX_WIDTH - ATTN_WIDTH
POOL_GROUP_DIM = POOL_WIDTH // POOL_GROUPS
AB_IN_WIDTH = 3 * ATTN_WIDTH + POOL_WIDTH
SGU_CHUNK = 128
SGU_GROUPS = 8
SGU_WIDTH = MIX_WIDTH
SGU_GROUP_DIM = SGU_WIDTH // SGU_GROUPS
PEER_HEADS = 8
PEER_N_KEYS = 128
PEER_EXPERTS = PEER_N_KEYS * PEER_N_KEYS
PEER_TOPK = 16
PEER_QUERY_DIM = 256
PEER_HALF_DIM = PEER_QUERY_DIM // 2
PEER_TOKEN_BLOCK = 512
N_EVEN = (DEPTH + 1) // 2
N_ODD = DEPTH // 2
RMS_EPS = 1e-6
NEG_INF = -1e30

kernel_name = "hybrid_dilated_pool_sgu_peer_encoder"


def rms_norm(x, g):
    x32 = x.astype(jnp.float32)
    y = x32 * lax.rsqrt(jnp.mean(x32 * x32, axis=-1, keepdims=True) + RMS_EPS)
    return (y * g.astype(jnp.float32)).astype(x.dtype)


def partial_rotary(t):
    S = t.shape[1]
    half = ROPE_DIM // 2
    pos = jnp.arange(S, dtype=jnp.float32)
    inv_freq = ROPE_THETA ** (-jnp.arange(0, ROPE_DIM, 2, dtype=jnp.float32) / ROPE_DIM)
    ang = pos[:, None] * inv_freq[None, :]
    cos = jnp.cos(ang)[None, :, None, :]
    sin = jnp.sin(ang)[None, :, None, :]
    tr = t[..., :ROPE_DIM].astype(jnp.float32)
    t1, t2 = tr[..., :half], tr[..., half:]
    rot = jnp.concatenate([t1 * cos - t2 * sin, t2 * cos + t1 * sin], axis=-1)
    return jnp.concatenate([rot.astype(t.dtype), t[..., ROPE_DIM:]], axis=-1)


def banded_attention(q, k, v, half):
    N, L, H, dh = q.shape
    bq = math.gcd(L, QUERY_BLOCK)
    nb = L // bq
    kw = bq + 2 * half
    pad = ((0, 0), (half, half), (0, 0), (0, 0))
    kp = jnp.pad(k, pad)
    vp = jnp.pad(v, pad)
    kidx = (jnp.arange(nb) * bq)[:, None] + jnp.arange(kw)[None, :]
    kb = kp[:, kidx]
    vb = vp[:, kidx]
    qb = q.reshape(N, nb, bq, H, dh)
    s = jnp.einsum('nbqhd,nbkhd->nbhqk', qb, kb).astype(jnp.float32) * (dh ** -0.5)
    rel = jnp.arange(kw)[None, :] - half - jnp.arange(bq)[:, None]
    kpos = kidx - half
    valid = (jnp.abs(rel) <= half)[None] & ((kpos >= 0) & (kpos < L))[:, None, :]
    s = jnp.where(valid[None, :, None], s, NEG_INF)
    lse = jax.nn.logsumexp(s, axis=-1)
    p = jnp.exp(s - lse[..., None])
    o = jnp.einsum('nbhqk,nbkhd->nbqhd', p, vb.astype(jnp.float32))
    return o.reshape(N, L, H, dh), lse.transpose(0, 1, 3, 2).reshape(N, L, H)


def dilated_attention(q, k, v):
    B, S, H, dh = q.shape
    outs, lses = [], []
    for window, dil in DILATED_PATTERNS:
        half = window // (2 * dil)
        L = S // dil

        def strided(t):
            return t.reshape(B, L, dil, H, dh).transpose(0, 2, 1, 3, 4).reshape(B * dil, L, H, dh)

        o, lse = banded_attention(strided(q), strided(k), strided(v), half)
        outs.append(o.reshape(B, dil, L, H, dh).transpose(0, 2, 1, 3, 4).reshape(B, S, H, dh))
        lses.append(lse.reshape(B, dil, L, H).transpose(0, 2, 1, 3).reshape(B, S, H))
    w = jax.nn.softmax(jnp.stack(lses, axis=-1), axis=-1)
    out = jnp.einsum('bshdp,bshp->bshd', jnp.stack(outs, axis=-1), w)
    return out.reshape(B, S, H * dh).astype(q.dtype)


def multiscale_pool(u, w, scale):
    B, S, _ = u.shape
    ug = u.astype(jnp.float32).reshape(B, S, POOL_GROUPS, POOL_GROUP_DIM)
    c = jnp.pad(jnp.cumsum(ug, axis=1), ((0, 0), (1, 0), (0, 0), (0, 0)))
    pos = jnp.arange(S)
    pooled = []
    for g, win in enumerate(POOL_WINDOWS):
        lo = jnp.clip(pos - win // 2, 0, S)
        hi = jnp.clip(pos + win - win // 2, 0, S)
        cnt = (hi - lo).astype(jnp.float32)
        pooled.append((c[:, hi, g] - c[:, lo, g]) / cnt[None, :, None])
    pooled = jnp.stack(pooled, axis=2)
    mixed = jnp.einsum('bsgc,gcd->bsgd', pooled - ug, w.astype(jnp.float32))
    return (mixed.reshape(B, S, POOL_WIDTH) * scale.astype(jnp.float32)).astype(u.dtype)


def spatial_gating_mlp(h, w_in, norm_g, w_s, b_s, w_out):
    B, S, _ = h.shape
    z = jax.nn.gelu(h @ w_in)
    u, v = z[..., :SGU_WIDTH], z[..., SGU_WIDTH:]
    v = rms_norm(v, norm_g)
    nc = S // SGU_CHUNK
    vc = v.reshape(B, nc, SGU_CHUNK, SGU_GROUPS, SGU_GROUP_DIM)
    mixed = jnp.einsum('gpq,bnqgc->bnpgc', w_s, vc) + b_s.T[None, None, :, :, None]
    gated = u * mixed.reshape(B, S, SGU_WIDTH)
    return gated @ w_out


def peer(h, w_q, sub_keys, w_down, w_up):
    B, S, D = h.shape
    T = B * S
    ht = h.reshape(T, D)
    q = (ht @ w_q).reshape(T, PEER_HEADS, 2, PEER_HALF_DIM)
    scores = jnp.einsum('thic,hikc->thik', q, sub_keys).astype(jnp.float32)
    sv, si = lax.top_k(scores, PEER_TOPK)
    cand = sv[:, :, 0, :, None] + sv[:, :, 1, None, :]
    best, ci = lax.top_k(cand.reshape(T, PEER_HEADS, PEER_TOPK * PEER_TOPK), PEER_TOPK)
    i1 = jnp.take_along_axis(si[:, :, 0], ci // PEER_TOPK, axis=-1)
    i2 = jnp.take_along_axis(si[:, :, 1], ci % PEER_TOPK, axis=-1)
    expert = (i1 * PEER_N_KEYS + i2).reshape(T, PEER_HEADS * PEER_TOPK)
    gate = jax.nn.softmax(best, axis=-1).reshape(T, PEER_HEADS * PEER_TOPK)
    blk = math.gcd(T, PEER_TOKEN_BLOCK)
    nblk = T // blk

    def expert_block(args):
        xb, eb, gb = args
        hid = jax.nn.gelu(jnp.einsum('cd,ckd->ck', xb, w_down[eb]).astype(jnp.float32))
        return jnp.einsum('ck,ckd->cd', (gb * hid).astype(xb.dtype), w_up[eb])

    out = lax.map(expert_block, (ht.reshape(nblk, blk, D),
                                 expert.reshape(nblk, blk, PEER_HEADS * PEER_TOPK),
                                 gate.reshape(nblk, blk, PEER_HEADS * PEER_TOPK)))
    return out.reshape(B, S, D)


def setup_inputs(seed: int = 0) -> dict:
    key = jax.random.key(seed)
    ks = jax.random.split(key, 20)
    f32 = jnp.float32

    def nrm(k, shape, scale):
        return jax.random.normal(k, shape, f32) * scale

    return {
        "x": nrm(ks[0], (BATCH, SEQ, D_MODEL), 1.0),
        "norm_mix": 1.0 + nrm(ks[1], (DEPTH, D_MODEL), 0.02),
        "norm_ffn": 1.0 + nrm(ks[2], (DEPTH, D_MODEL), 0.02),
        "norm_final": 1.0 + nrm(ks[3], (D_MODEL,), 0.02),
        "ab_w_in": nrm(ks[4], (N_EVEN, D_MODEL, AB_IN_WIDTH), D_MODEL ** -0.5),
        "ab_w_out": nrm(ks[5], (N_EVEN, MIX_WIDTH, D_MODEL), MIX_WIDTH ** -0.5),
        "pool_w": nrm(ks[6], (N_EVEN, POOL_GROUPS, POOL_GROUP_DIM, POOL_GROUP_DIM), POOL_GROUP_DIM ** -0.5),
        "pool_scale": 1.0 + nrm(ks[7], (N_EVEN, POOL_WIDTH), 0.02),
        "c_w_in": nrm(ks[8], (N_ODD, D_MODEL, 2 * SGU_WIDTH), D_MODEL ** -0.5),
        "c_norm": 1.0 + nrm(ks[9], (N_ODD, SGU_WIDTH), 0.02),
        "c_w_s": nrm(ks[10], (N_ODD, SGU_GROUPS, SGU_CHUNK, SGU_CHUNK), SGU_CHUNK ** -0.5),
        "c_b_s": 1.0 + nrm(ks[11], (N_ODD, SGU_GROUPS, SGU_CHUNK), 0.02),
        "c_w_out": nrm(ks[12], (N_ODD, SGU_WIDTH, D_MODEL), SGU_WIDTH ** -0.5),
        "peer_w_q": nrm(ks[13], (DEPTH, D_MODEL, PEER_HEADS * PEER_QUERY_DIM), D_MODEL ** -0.5),
        "peer_sub_keys": nrm(ks[14], (DEPTH, PEER_HEADS, 2, PEER_N_KEYS, PEER_HALF_DIM), PEER_HALF_DIM ** -0.5),
        "peer_w_down": nrm(ks[15], (DEPTH, PEER_EXPERTS, D_MODEL), D_MODEL ** -0.5),
        "peer_w_up": nrm(ks[16], (DEPTH, PEER_EXPERTS, D_MODEL), PEER_HEADS ** -0.5),
    }


def reference(x, norm_mix, norm_ffn, norm_final, ab_w_in, ab_w_out, pool_w, pool_scale,
              c_w_in, c_norm, c_w_s, c_b_s, c_w_out,
              peer_w_q, peer_sub_keys, peer_w_down, peer_w_up):
    B, S, _ = x.shape
    for layer in range(DEPTH):
        j = layer // 2
        h = rms_norm(x, norm_mix[layer])
        if layer % 2 == 0:
            proj = h @ ab_w_in[j]
            q = proj[..., :ATTN_WIDTH].reshape(B, S, ATTN_HEADS, HEAD_DIM)
            k = proj[..., ATTN_WIDTH:2 * ATTN_WIDTH].reshape(B, S, ATTN_HEADS, HEAD_DIM)
            v = proj[..., 2 * ATTN_WIDTH:3 * ATTN_WIDTH].reshape(B, S, ATTN_HEADS, HEAD_DIM)
            p_in = proj[..., 3 * ATTN_WIDTH:]
            a_out = dilated_attention(partial_rotary(q), partial_rotary(k), v)
            b_out = multiscale_pool(p_in, pool_w[j], pool_scale[j])
            mix = jnp.concatenate([a_out, b_out], axis=-1) @ ab_w_out[j]
        else:
            mix = spatial_gating_mlp(h, c_w_in[j], c_norm[j], c_w_s[j], c_b_s[j], c_w_out[j])
        x = x + mix
        x = x + peer(rms_norm(x, norm_ffn[layer]), peer_w_q[layer], peer_sub_keys[layer],
                     peer_w_down[layer], peer_w_up[layer])
    return rms_norm(x, norm_final)
```

```python
import functools
import math

import jax
import jax.numpy as jnp
import numpy as np
from jax import lax
from jax.experimental import pallas as pl
from jax.experimental.pallas import tpu as pltpu

F32 = jnp.float32
BF16 = jnp.bfloat16

LANES = 128
SUBLANES = 8
VREG_TOKENS = SUBLANES * LANES
VMEM_LIMIT = 56 * 1024 * 1024

ATTN_HEADS = 8
HEAD_DIM = 64
ATTN_WIDTH = ATTN_HEADS * HEAD_DIM
DILATIONS = (1, 4, 16)
ATTN_HALF = 64
QUERY_BLOCK = 128
ROPE_THETA = 500000.0
ROPE_DIM = HEAD_DIM // 4
POOL_WINDOWS = (2, 4, 8, 16)
POOL_HALO = 8
SGU_CHUNK = 128
SGU_GROUPS = 8
PEER_HEADS = 8
PEER_N_KEYS = 128
PEER_TOPK = 16
RMS_EPS = 1e-6
NEG_INF = -1e30
GELU_C = math.sqrt(2.0 / math.pi)


def _params(semantics):
    return pltpu.CompilerParams(dimension_semantics=semantics, vmem_limit_bytes=VMEM_LIMIT)


def _rms(x, g):
    return x * lax.rsqrt(jnp.mean(x * x, axis=-1, keepdims=True) + RMS_EPS) * g


def _gelu(x):
    return x * (0.5 * (1.0 + jnp.tanh(GELU_C * (x + 0.044715 * (x * x * x)))))


def _inproj_kernel(x_ref, g_ref, w_ref, cos_ref, sa_ref, sb_ref, qkv_ref, p_ref):
    h = _rms(x_ref[...], g_ref[...]).astype(BF16)
    proj = jnp.dot(h, w_ref[...], preferred_element_type=F32)
    c, sa, sb = cos_ref[...], sa_ref[...], sb_ref[...]
    half = ROPE_DIM // 2
    for col in range(0, 2 * ATTN_WIDTH, LANES):
        t = proj[:, col:col + LANES]
        up = pltpu.roll(t, LANES - half, axis=1)
        dn = pltpu.roll(t, half, axis=1)
        qkv_ref[:, col:col + LANES] = (t * c + up * sa + dn * sb).astype(BF16)
    qkv_ref[:, 2 * ATTN_WIDTH:] = proj[:, 2 * ATTN_WIDTH:3 * ATTN_WIDTH].astype(BF16)
    p_ref[...] = proj[:, 3 * ATTN_WIDTH:]


def _rope_tables(seq):
    half = ROPE_DIM // 2
    pos = jnp.arange(seq, dtype=F32)
    inv_freq = ROPE_THETA ** (-jnp.arange(0, ROPE_DIM, 2, dtype=F32) / ROPE_DIM)
    ang = pos[:, None] * inv_freq[None, :]
    cos, sin = jnp.cos(ang), jnp.sin(ang)
    pad = HEAD_DIM - ROPE_DIM
    ones = jnp.ones((seq, pad), F32)
    zeros = jnp.zeros((seq, pad), F32)
    zh = jnp.zeros((seq, half), F32)
    c = jnp.concatenate([cos, cos, ones], axis=1)
    sa = jnp.concatenate([-sin, zh, zeros], axis=1)
    sb = jnp.concatenate([zh, sin, zeros], axis=1)
    return tuple(jnp.concatenate([t, t], axis=1) for t in (c, sa, sb))


def _inproj(x2, g, w_in, seq, tm):
    T, D = x2.shape
    n_out = w_in.shape[1]
    pool_w = n_out - 3 * ATTN_WIDTH
    c, sa, sb = _rope_tables(seq)
    spt = seq // tm
    tab = pl.BlockSpec((tm, LANES), lambda i: (i % spt, 0))
    return pl.pallas_call(
        _inproj_kernel,
        grid=(T // tm,),
        in_specs=[pl.BlockSpec((tm, D), lambda i: (i, 0)),
                  pl.BlockSpec((1, D), lambda i: (0, 0)),
                  pl.BlockSpec((D, n_out), lambda i: (0, 0)),
                  tab, tab, tab],
        out_specs=[pl.BlockSpec((tm, 3 * ATTN_WIDTH), lambda i: (i, 0)),
                   pl.BlockSpec((tm, pool_w), lambda i: (i, 0))],
        out_shape=[jax.ShapeDtypeStruct((T, 3 * ATTN_WIDTH), BF16),
                   jax.ShapeDtypeStruct((T, pool_w), F32)],
        compiler_params=_params(("parallel",)),
    )(x2, g.reshape(1, D), w_in.astype(BF16), c, sa, sb)


def _attn_kernel(q_ref, k_ref, v_ref, o_ref, lse_ref, *, length):
    kw = min(QUERY_BLOCK + 2 * ATTN_HALF, length)
    scale = HEAD_DIM ** -0.5
    lane = lax.broadcasted_iota(jnp.int32, (QUERY_BLOCK, LANES), 1)
    first_head = lane < HEAD_DIM

    def block(qb, carry):
        q0 = pl.multiple_of(qb * QUERY_BLOCK, QUERY_BLOCK)
        start = pl.multiple_of(jnp.clip(q0 - ATTN_HALF, 0, length - kw), ATTN_HALF)
        qpos = q0 + lax.broadcasted_iota(jnp.int32, (QUERY_BLOCK, kw), 0)
        kpos = start + lax.broadcasted_iota(jnp.int32, (QUERY_BLOCK, kw), 1)
        valid = jnp.abs(kpos - qpos) <= ATTN_HALF
        for col in range(0, ATTN_WIDTH, LANES):
            qp = q_ref[pl.ds(q0, QUERY_BLOCK), col:col + LANES]
            kp = k_ref[pl.ds(start, kw), col:col + LANES]
            vp = v_ref[pl.ds(start, kw), col:col + LANES]
            outs, lses = [], []
            for mask in (first_head, jnp.logical_not(first_head)):
                qm = jnp.where(mask, qp, jnp.zeros_like(qp))
                s = lax.dot_general(qm, kp, (((1,), (1,)), ((), ())),
                                    preferred_element_type=F32) * scale
                s = jnp.where(valid, s, NEG_INF)
                m = jnp.max(s, axis=-1, keepdims=True)
                p = jnp.exp(s - m)
                l = jnp.sum(p, axis=-1, keepdims=True)
                outs.append(jnp.dot(p.astype(BF16), vp, preferred_element_type=F32) / l)
                lses.append(jnp.broadcast_to(m + jnp.log(l), (QUERY_BLOCK, LANES)))
            o_ref[pl.ds(q0, QUERY_BLOCK), col:col + LANES] = (
                jnp.where(first_head, outs[0], outs[1]).astype(BF16))
            lse_ref[pl.ds(q0, QUERY_BLOCK), col:col + LANES] = (
                jnp.where(first_head, lses[0], lses[1]))
        return carry

    lax.fori_loop(0, length // QUERY_BLOCK, block, 0)


def _attention(qkv, batch, seq, dil):
    length = seq // dil
    width = 3 * ATTN_WIDTH
    view = qkv.reshape(batch, length, dil * width)

    def part(j):
        return pl.BlockSpec((None, length, ATTN_WIDTH), lambda b, r: (b, 0, 3 * r + j))

    out = pl.BlockSpec((None, length, ATTN_WIDTH), lambda b, r: (b, 0, r))
    o, lse = pl.pallas_call(
        functools.partial(_attn_kernel, length=length),
        grid=(batch, dil),
        in_specs=[part(0), part(1), part(2)],
        out_specs=[out, out],
        out_shape=[jax.ShapeDtypeStruct((batch, length, dil * ATTN_WIDTH), BF16),
                   jax.ShapeDtypeStruct((batch, length, dil * ATTN_WIDTH), F32)],
        compiler_params=_params(("parallel", "parallel")),
    )(view, view, view)
    return o.reshape(batch * seq, ATTN_WIDTH), lse.reshape(batch * seq, ATTN_WIDTH)


def _mix0_kernel(o1, o2, o3, l1, l2, l3, pc_ref, pp_ref, pn_ref, x_ref, wout_ref, pw_ref, ps_ref,
                 out_ref, ext_ref, *, seq, tm):
    ti = pl.program_id(0) % (seq // tm)
    la, lb, lc = l1[...], l2[...], l3[...]
    m = jnp.maximum(jnp.maximum(la, lb), lc)
    ea, eb, ec = jnp.exp(la - m), jnp.exp(lb - m), jnp.exp(lc - m)
    a = (o1[...].astype(F32) * ea + o2[...].astype(F32) * eb + o3[...].astype(F32) * ec) / (ea + eb + ec)
    acc = jnp.dot(a.astype(BF16), wout_ref[:ATTN_WIDTH, :], preferred_element_type=F32)

    ext_ref[:POOL_HALO, :] = jnp.where(ti > 0, pp_ref[...], 0.0)
    ext_ref[POOL_HALO:POOL_HALO + tm, :] = pc_ref[...]
    ext_ref[POOL_HALO + tm:, :] = jnp.where(ti < seq // tm - 1, pn_ref[...], 0.0)
    pos = ti * tm + lax.broadcasted_iota(jnp.int32, (tm, 1), 0)
    for g, win in enumerate(POOL_WINDOWS):
        cols = slice(g * LANES, (g + 1) * LANES)
        before, after = win // 2, win - win // 2
        tot = ext_ref[POOL_HALO - before:POOL_HALO - before + tm, cols]
        for d in range(-before + 1, after):
            tot = tot + ext_ref[POOL_HALO + d:POOL_HALO + d + tm, cols]
        cnt = (jnp.minimum(pos + after, seq) - jnp.maximum(pos - before, 0)).astype(F32)
        diff = tot / cnt - pc_ref[:, cols]
        mixed = jnp.dot(diff.astype(BF16), pw_ref[g], preferred_element_type=F32) * ps_ref[:, cols]
        acc = acc + jnp.dot(mixed.astype(BF16),
                            wout_ref[ATTN_WIDTH + g * LANES:ATTN_WIDTH + (g + 1) * LANES, :],
                            preferred_element_type=F32)
    out_ref[...] = x_ref[...] + acc


def _mix0(outs, lses, p_in, x2, w_out, pool_w, pool_scale, seq, tm):
    T, D = x2.shape
    pw = p_in.shape[1]
    hb = tm // POOL_HALO
    last = T // POOL_HALO - 1
    row = lambda w: pl.BlockSpec((tm, w), lambda i: (i, 0))
    full = lambda s: pl.BlockSpec(s, lambda i: (0,) * len(s))
    return pl.pallas_call(
        functools.partial(_mix0_kernel, seq=seq, tm=tm),
        grid=(T // tm,),
        in_specs=[row(ATTN_WIDTH)] * 6 + [
            row(pw),
            pl.BlockSpec((POOL_HALO, pw), lambda i: (jnp.maximum(i * hb - 1, 0), 0)),
            pl.BlockSpec((POOL_HALO, pw), lambda i: (jnp.minimum((i + 1) * hb, last), 0)),
            row(D), full(w_out.shape), full(pool_w.shape), full((1, pw))],
        out_specs=row(D),
        out_shape=jax.ShapeDtypeStruct((T, D), F32),
        scratch_shapes=[pltpu.VMEM((tm + 2 * POOL_HALO, pw), F32)],
        compiler_params=_params(("parallel",)),
    )(*outs, *lses, p_in, p_in, p_in, x2, w_out.astype(BF16), pool_w.astype(BF16),
      pool_scale.reshape(1, pw))


def _sgu_kernel(x_ref, g_ref, win_ref, cn_ref, ws_ref, bs_ref, wout_ref, out_ref, gated_ref, *, tm):
    x = x_ref[...]
    width = wout_ref.shape[0]
    h = _rms(x, g_ref[...]).astype(BF16)
    u = _gelu(jnp.dot(h, win_ref[:, :width], preferred_element_type=F32))
    v = _gelu(jnp.dot(h, win_ref[:, width:], preferred_element_type=F32))
    vb = _rms(v, cn_ref[...]).astype(BF16)
    for n in range(tm // SGU_CHUNK):
        rows = slice(n * SGU_CHUNK, (n + 1) * SGU_CHUNK)
        for g in range(SGU_GROUPS):
            cols = slice(g * LANES, (g + 1) * LANES)
            mixed = jnp.dot(ws_ref[g], vb[rows, cols], preferred_element_type=F32) + bs_ref[:, cols]
            gated_ref[rows, cols] = (u[rows, cols] * mixed).astype(BF16)
    out_ref[...] = x + jnp.dot(gated_ref[...], wout_ref[...], preferred_element_type=F32)


def _sgu(x2, g, w_in, c_norm, w_s, b_s, w_out, tm):
    T, D = x2.shape
    width = w_out.shape[0]
    bias = jnp.repeat(b_s.T, width // SGU_GROUPS, axis=1)
    full = lambda s: pl.BlockSpec(s, lambda i: (0,) * len(s))
    return pl.pallas_call(
        functools.partial(_sgu_kernel, tm=tm),
        grid=(T // tm,),
        in_specs=[pl.BlockSpec((tm, D), lambda i: (i, 0)), full((1, D)), full(w_in.shape),
                  full((1, width)), full(w_s.shape), full(bias.shape), full(w_out.shape)],
        out_specs=pl.BlockSpec((tm, D), lambda i: (i, 0)),
        out_shape=jax.ShapeDtypeStruct((T, D), F32),
        scratch_shapes=[pltpu.VMEM((tm, width), BF16)],
        compiler_params=_params(("parallel",)),
    )(x2, g.reshape(1, D), w_in.astype(BF16), c_norm.reshape(1, width), w_s.astype(BF16), bias,
      w_out.astype(BF16))


def _cmpx(v, i, l, descending):
    hi, lo = jnp.maximum(v[i], v[l]), jnp.minimum(v[i], v[l])
    v[i], v[l] = (hi, lo) if descending else (lo, hi)


def _bitonic_sort_desc(v):
    v = list(v)
    n = len(v)
    k = 2
    while k <= n:
        j = k // 2
        while j >= 1:
            for i in range(n):
                l = i ^ j
                if l > i:
                    _cmpx(v, i, l, (i & k) == 0)
            j //= 2
        k *= 2
    return v


def _merge_top(a, b):
    n = len(a)
    v = [jnp.maximum(a[i], b[n - 1 - i]) for i in range(n)]
    j = n // 2
    while j >= 1:
        for i in range(n):
            if (i & j) == 0:
                _cmpx(v, i, i + j, True)
        j //= 2
    return v


def _top_sorted(vals, k):
    groups = [_bitonic_sort_desc(vals[i:i + k]) for i in range(0, len(vals), k)]
    while len(groups) > 1:
        groups = [_merge_top(groups[i], groups[i + 1]) for i in range(0, len(groups), 2)]
    return groups[0]


def _count(preds):
    tot = jnp.where(preds[0], 1.0, 0.0)
    for p in preds[1:]:
        tot = tot + jnp.where(p, 1.0, 0.0)
    return tot


def _peer_gate_kernel(x_ref, g_ref, wq_ref, keys_ref, ht_ref, r2_ref, e2_ref, ne_ref,
                      km_ref, om_ref, *, tg):
    K = PEER_TOPK
    nk = PEER_N_KEYS
    chunks = tg // LANES

    @pl.when(pl.program_id(1) == 0)
    def _():
        ht_ref[...] = _rms(x_ref[...], g_ref[...]).T.astype(BF16)

    qt = jnp.dot(wq_ref[...], ht_ref[...], preferred_element_type=F32)
    half_dim = qt.shape[0] // 2
    for side in range(2):
        sc = jnp.dot(keys_ref[side], qt[side * half_dim:(side + 1) * half_dim].astype(BF16),
                     preferred_element_type=F32)
        for c in range(chunks):
            km_ref[side, pl.ds(c, nk, stride=chunks), :] = sc[:, c * LANES:(c + 1) * LANES]

    def keyrows(k):
        return slice(k * chunks, (k + 1) * chunks)

    s1 = [km_ref[0, keyrows(k), :] for k in range(nk)]
    s2 = [km_ref[1, keyrows(k), :] for k in range(nk)]
    A = _top_sorted(s1, K)
    B = _top_sorted(s2, K)

    reach = [K // (i + 1) for i in range(K)]
    cand = [[A[i] + B[j] for j in range(reach[i])] for i in range(K)]
    ninf = jnp.full_like(A[0], -jnp.inf)
    rest = [c for row in cand[1:] for c in row]
    rest = rest + [ninf] * (-len(rest) % K)
    top = cand[0]
    for i in range(0, len(rest), K):
        top = _merge_top(top, _bitonic_sort_desc(rest[i:i + K]))
    thr = top[K - 1]

    gt = [_count([c > thr for c in row]) for row in cand]
    eq = [_count([c == thr for c in row]) for row in cand]
    total_gt = functools.reduce(lambda a, b: a + b, gt)
    room = float(K) - total_gt
    n = []
    for i in range(K):
        n.append(gt[i] + jnp.clip(room, 0.0, eq[i]))
        room = room - eq[i]

    e1 = [jnp.exp(a - A[0]) for a in A]
    e2 = [jnp.exp(b - B[0]) for b in B]
    z = jnp.zeros_like(A[0])
    for i in range(K):
        inner = jnp.zeros_like(z)
        for j in range(reach[i]):
            inner = inner + jnp.where(n[i] > float(j), e2[j], 0.0)
        z = z + e1[i] * inner
    inv_z = 1.0 / z

    pinf = jnp.full_like(A[0], jnp.inf)
    t = []
    for m in range(1, K + 1):
        tm_ = pinf
        for i in range(K):
            tm_ = jnp.minimum(tm_, jnp.where(n[i] >= float(m), A[i], pinf))
        t.append(tm_)

    for k in range(nk):
        om_ref[0, keyrows(k), :] = _count([s1[k] >= tv for tv in t])
        om_ref[1, keyrows(k), :] = jnp.exp(s1[k] - A[0]) * inv_z
        om_ref[2, keyrows(k), :] = _count([b > s2[k] for b in B])
        om_ref[3, keyrows(k), :] = jnp.exp(s2[k] - B[0])

    for c in range(chunks):
        cols = slice(c * LANES, (c + 1) * LANES)
        ne_ref[0, :, cols] = om_ref[0, pl.ds(c, nk, stride=chunks), :]
        ne_ref[1, :, cols] = om_ref[1, pl.ds(c, nk, stride=chunks), :]
        r2_ref[:, cols] = om_ref[2, pl.ds(c, nk, stride=chunks), :].astype(BF16)
        e2_ref[:, cols] = om_ref[3, pl.ds(c, nk, stride=chunks), :].astype(BF16)


def _peer_gates(x2, g, w_q, sub_keys):
    T, D = x2.shape
    tg = VREG_TOKENS
    heads, _, nk, half_dim = sub_keys.shape
    wq_t = w_q.T.astype(BF16)
    return pl.pallas_call(
        functools.partial(_peer_gate_kernel, tg=tg),
        grid=(T // tg, heads),
        in_specs=[pl.BlockSpec((tg, D), lambda i, h: (i, 0)),
                  pl.BlockSpec((1, D), lambda i, h: (0, 0)),
                  pl.BlockSpec((2 * half_dim, D), lambda i, h: (h, 0)),
                  pl.BlockSpec((None, 2, nk, half_dim), lambda i, h: (h, 0, 0, 0))],
        out_specs=[pl.BlockSpec((D, tg), lambda i, h: (0, i)),
                   pl.BlockSpec((nk, tg), lambda i, h: (h, i)),
                   pl.BlockSpec((nk, tg), lambda i, h: (h, i)),
                   pl.BlockSpec((None, 2, nk, tg), lambda i, h: (h, 0, 0, i))],
        out_shape=[jax.ShapeDtypeStruct((D, T), BF16),
                   jax.ShapeDtypeStruct((heads * nk, T), BF16),
                   jax.ShapeDtypeStruct((heads * nk, T), BF16),
                   jax.ShapeDtypeStruct((heads, 2, nk, T), F32)],
        scratch_shapes=[pltpu.VMEM((2, nk * SUBLANES, LANES), F32),
                        pltpu.VMEM((4, nk * SUBLANES, LANES), F32)],
        compiler_params=_params(("parallel", "arbitrary")),
    )(x2, g.reshape(1, D), wq_t, sub_keys.astype(BF16))


def _peer_expert_kernel(ht_ref, wd_ref, wu_ref, r2_ref, e2_ref, ne_ref, x_ref, gf_ref, out_ref,
                        acc_ref, ga_ref, *, final_norm):
    j = pl.program_id(1)
    nk = PEER_N_KEYS
    keys_per_step = wd_ref.shape[0] // nk

    @pl.when(j == 0)
    def _():
        acc_ref[...] = jnp.zeros_like(acc_ref)

    hid = jnp.dot(wd_ref[...], ht_ref[...], preferred_element_type=F32)
    for al in range(keys_per_step):
        a = j * keys_per_step + al
        rows = slice(al * nk, (al + 1) * nk)
        act = _gelu(hid[rows, :]).astype(BF16)
        gate = jnp.zeros(act.shape, BF16)
        for h in range(PEER_HEADS):
            hrows = slice(h * nk, (h + 1) * nk)
            count = ne_ref[h, 0, pl.ds(a, 1), :].astype(BF16)
            factor = ne_ref[h, 1, pl.ds(a, 1), :].astype(BF16)
            chosen = r2_ref[hrows, :] < count
            gate = gate + jnp.where(chosen, e2_ref[hrows, :], jnp.zeros_like(gate)) * factor
        ga_ref[rows, :] = gate * act
    acc_ref[...] += jnp.dot(wu_ref[...], ga_ref[...], preferred_element_type=F32)

    @pl.when(j == pl.num_programs(1) - 1)
    def _():
        y = x_ref[...] + acc_ref[...].T
        out_ref[...] = _rms(y, gf_ref[...]) if final_norm else y


def _peer_experts(ht, r2, e2, ne, x2, w_down, w_up, g_final, final_norm, tm, eb):
    T, D = x2.shape
    n_exp = w_down.shape[0]
    heads, _, nk, _ = ne.shape
    return pl.pallas_call(
        functools.partial(_peer_expert_kernel, final_norm=final_norm),
        grid=(T // tm, n_exp // eb),
        in_specs=[pl.BlockSpec((D, tm), lambda i, j: (0, i)),
                  pl.BlockSpec((eb, D), lambda i, j: (j, 0)),
                  pl.BlockSpec((D, eb), lambda i, j: (0, j)),
                  pl.BlockSpec((heads * nk, tm), lambda i, j: (0, i)),
                  pl.BlockSpec((heads * nk, tm), lambda i, j: (0, i)),
                  pl.BlockSpec((heads, 2, nk, tm), lambda i, j: (0, 0, 0, i)),
                  pl.BlockSpec((tm, D), lambda i, j: (i, 0)),
                  pl.BlockSpec((1, D), lambda i, j: (0, 0))],
        out_specs=pl.BlockSpec((tm, D), lambda i, j: (i, 0)),
        out_shape=jax.ShapeDtypeStruct((T, D), F32),
        scratch_shapes=[pltpu.VMEM((D, tm), F32), pltpu.VMEM((eb, tm), BF16)],
        compiler_params=_params(("parallel", "arbitrary")),
    )(ht, w_down.astype(BF16), w_up.T.astype(BF16), r2, e2, ne, x2, g_final.reshape(1, D))


def _peer(x2, g, w_q, sub_keys, w_down, w_up, g_final, final_norm, tm, eb):
    ht, r2, e2, ne = _peer_gates(x2, g, w_q, sub_keys)
    return _peer_experts(ht, r2, e2, ne, x2, w_down, w_up, g_final, final_norm, tm, eb)


def kernel(x, norm_mix, norm_ffn, norm_final, ab_w_in, ab_w_out, pool_w, pool_scale, c_w_in,
           c_norm, c_w_s, c_b_s, c_w_out, peer_w_q, peer_sub_keys, peer_w_down, peer_w_up):
    B, S, D = x.shape
    depth = norm_mix.shape[0]
    tm = 512
    eb = 512
    assert S % tm == 0 and (B * S) % VREG_TOKENS == 0
    assert all(S % (d * QUERY_BLOCK) == 0 for d in DILATIONS)
    x2 = x.reshape(B * S, D)
    for layer in range(depth):
        j = layer // 2
        if layer % 2 == 0:
            qkv, p_in = _inproj(x2, norm_mix[layer], ab_w_in[j], S, tm)
            outs, lses = zip(*[_attention(qkv, B, S, dil) for dil in DILATIONS])
            x2 = _mix0(outs, lses, p_in, x2, ab_w_out[j], pool_w[j], pool_scale[j], S, tm)
        else:
            x2 = _sgu(x2, norm_mix[layer], c_w_in[j], c_norm[j], c_w_s[j], c_b_s[j], c_w_out[j], tm)
        x2 = _peer(x2, norm_ffn[layer], peer_w_q[layer], peer_sub_keys[layer], peer_w_down[layer],
                   peer_w_up[layer], norm_final, layer == depth - 1, tm, eb)
    return x2.reshape(B, S, D)
```

```python
import functools
import math

import jax
import jax.numpy as jnp
import numpy as np
from jax import lax
from jax.experimental import pallas as pl
from jax.experimental.pallas import tpu as pltpu

F32 = jnp.float32
BF16 = jnp.bfloat16

LANES = 128
SUBLANES = 8
BF16_ROWS = 2 * SUBLANES
MXU_WIDTH = 256
VREG_TOKENS = SUBLANES * LANES
VMEM_LIMIT = 56 * 1024 * 1024

ATTN_HEADS = 8
HEAD_DIM = 64
ATTN_WIDTH = ATTN_HEADS * HEAD_DIM
DILATIONS = (1, 4, 16)
ATTN_HALF = 64
QUERY_BLOCK = 128
ROPE_THETA = 500000.0
ROPE_DIM = HEAD_DIM // 4
POOL_WINDOWS = (2, 4, 8, 16)
POOL_HALO = 8
SGU_CHUNK = 128
SGU_GROUPS = 8
PEER_HEADS = 8
PEER_N_KEYS = 128
PEER_TOPK = 16
RMS_EPS = 1e-6
NEG_INF = -1e30
GELU_C = math.sqrt(2.0 / math.pi)


def _params(semantics):
    return pltpu.CompilerParams(dimension_semantics=semantics, vmem_limit_bytes=VMEM_LIMIT)


def _rms(x, g):
    return x * lax.rsqrt(jnp.mean(x * x, axis=-1, keepdims=True) + RMS_EPS) * g


U32 = jnp.uint32


def _pack_rows(w):
    wb = w.astype(BF16)
    return lax.bitcast_convert_type(jnp.stack([wb[0::2], wb[1::2]], axis=-1), U32)


def _as_bf16(words):
    return pltpu.bitcast(words, BF16)


def _as_words(x):
    return pltpu.bitcast(x.astype(BF16), U32)


def _gelu(x):
    return x * (0.5 * (1.0 + jnp.tanh(GELU_C * (x + 0.044715 * (x * x * x)))))


def _inproj_kernel(x_ref, g_ref, w_ref, cos_ref, sa_ref, sb_ref, qkv_ref, p_ref):
    h = _rms(x_ref[...], g_ref[...]).astype(BF16)
    proj = jnp.dot(h, w_ref[...], preferred_element_type=F32)
    c, sa, sb = cos_ref[...], sa_ref[...], sb_ref[...]
    half = ROPE_DIM // 2
    for col in range(0, 2 * ATTN_WIDTH, LANES):
        t = proj[:, col:col + LANES]
        up = pltpu.roll(t, LANES - half, axis=1)
        dn = pltpu.roll(t, half, axis=1)
        qkv_ref[:, col:col + LANES] = (t * c + up * sa + dn * sb).astype(BF16)
    qkv_ref[:, 2 * ATTN_WIDTH:] = proj[:, 2 * ATTN_WIDTH:3 * ATTN_WIDTH].astype(BF16)
    p_ref[...] = proj[:, 3 * ATTN_WIDTH:]


def _rope_tables(seq):
    half = ROPE_DIM // 2
    pos = jnp.arange(seq, dtype=F32)
    inv_freq = ROPE_THETA ** (-jnp.arange(0, ROPE_DIM, 2, dtype=F32) / ROPE_DIM)
    ang = pos[:, None] * inv_freq[None, :]
    cos, sin = jnp.cos(ang), jnp.sin(ang)
    pad = HEAD_DIM - ROPE_DIM
    ones = jnp.ones((seq, pad), F32)
    zeros = jnp.zeros((seq, pad), F32)
    zh = jnp.zeros((seq, half), F32)
    c = jnp.concatenate([cos, cos, ones], axis=1)
    sa = jnp.concatenate([-sin, zh, zeros], axis=1)
    sb = jnp.concatenate([zh, sin, zeros], axis=1)
    return tuple(jnp.concatenate([t, t], axis=1) for t in (c, sa, sb))


def _inproj(x2, g, w_in, seq, tm):
    T, D = x2.shape
    n_out = w_in.shape[1]
    pool_w = n_out - 3 * ATTN_WIDTH
    c, sa, sb = _rope_tables(seq)
    spt = seq // tm
    tab = pl.BlockSpec((tm, LANES), lambda i: (i % spt, 0))
    return pl.pallas_call(
        _inproj_kernel,
        grid=(T // tm,),
        in_specs=[pl.BlockSpec((tm, D), lambda i: (i, 0)),
                  pl.BlockSpec((1, D), lambda i: (0, 0)),
                  pl.BlockSpec((D, n_out), lambda i: (0, 0)),
                  tab, tab, tab],
        out_specs=[pl.BlockSpec((tm, 3 * ATTN_WIDTH), lambda i: (i, 0)),
                   pl.BlockSpec((tm, pool_w), lambda i: (i, 0))],
        out_shape=[jax.ShapeDtypeStruct((T, 3 * ATTN_WIDTH), BF16),
                   jax.ShapeDtypeStruct((T, pool_w), F32)],
        compiler_params=_params(("parallel",)),
    )(x2, g.reshape(1, D), w_in.astype(BF16), c, sa, sb)


def _attn_kernel(q_ref, k_ref, v_ref, o_ref, lse_ref, *, length):
    kw = min(QUERY_BLOCK + 2 * ATTN_HALF, length)
    scale = HEAD_DIM ** -0.5
    lane = lax.broadcasted_iota(jnp.int32, (QUERY_BLOCK, LANES), 1)
    first_head = lane < HEAD_DIM

    def block(qb, carry):
        q0 = pl.multiple_of(qb * QUERY_BLOCK, QUERY_BLOCK)
        start = pl.multiple_of(jnp.clip(q0 - ATTN_HALF, 0, length - kw), ATTN_HALF)
        qpos = q0 + lax.broadcasted_iota(jnp.int32, (QUERY_BLOCK, kw), 0)
        kpos = start + lax.broadcasted_iota(jnp.int32, (QUERY_BLOCK, kw), 1)
        valid = jnp.abs(kpos - qpos) <= ATTN_HALF
        for col in range(0, ATTN_WIDTH, LANES):
            qp = q_ref[pl.ds(q0, QUERY_BLOCK), col:col + LANES]
            kp = k_ref[pl.ds(start, kw), col:col + LANES]
            vp = v_ref[pl.ds(start, kw), col:col + LANES]
            outs, lses = [], []
            for mask in (first_head, jnp.logical_not(first_head)):
                qm = jnp.where(mask, qp, jnp.zeros_like(qp))
                s = lax.dot_general(qm, kp, (((1,), (1,)), ((), ())),
                                    preferred_element_type=F32) * scale
                s = jnp.where(valid, s, NEG_INF)
                m = jnp.max(s, axis=-1, keepdims=True)
                p = jnp.exp(s - m)
                l = jnp.sum(p, axis=-1, keepdims=True)
                outs.append(jnp.dot(p.astype(BF16), vp, preferred_element_type=F32) / l)
                lses.append(jnp.broadcast_to(m + jnp.log(l), (QUERY_BLOCK, LANES)))
            o_ref[pl.ds(q0, QUERY_BLOCK), col:col + LANES] = (
                jnp.where(first_head, outs[0], outs[1]).astype(BF16))
            lse_ref[pl.ds(q0, QUERY_BLOCK), col:col + LANES] = (
                jnp.where(first_head, lses[0], lses[1]))
        return carry

    lax.fori_loop(0, length // QUERY_BLOCK, block, 0)


def _attention(qkv, batch, seq, dil):
    length = seq // dil
    width = 3 * ATTN_WIDTH
    view = qkv.reshape(batch, length, dil * width)

    def part(j):
        return pl.BlockSpec((None, length, ATTN_WIDTH), lambda b, r: (b, 0, 3 * r + j))

    out = pl.BlockSpec((None, length, ATTN_WIDTH), lambda b, r: (b, 0, r))
    o, lse = pl.pallas_call(
        functools.partial(_attn_kernel, length=length),
        grid=(batch, dil),
        in_specs=[part(0), part(1), part(2)],
        out_specs=[out, out],
        out_shape=[jax.ShapeDtypeStruct((batch, length, dil * ATTN_WIDTH), BF16),
                   jax.ShapeDtypeStruct((batch, length, dil * ATTN_WIDTH), F32)],
        compiler_params=_params(("parallel", "parallel")),
    )(view, view, view)
    return o.reshape(batch * seq, ATTN_WIDTH), lse.reshape(batch * seq, ATTN_WIDTH)


def _mix0_kernel(o1, o2, o3, l1, l2, l3, pc_ref, pp_ref, pn_ref, x_ref, wout_ref, pw_ref, ps_ref,
                 out_ref, ext_ref, *, seq, tm):
    ti = pl.program_id(0) % (seq // tm)
    la, lb, lc = l1[...], l2[...], l3[...]
    m = jnp.maximum(jnp.maximum(la, lb), lc)
    ea, eb, ec = jnp.exp(la - m), jnp.exp(lb - m), jnp.exp(lc - m)
    a = (o1[...].astype(F32) * ea + o2[...].astype(F32) * eb + o3[...].astype(F32) * ec) / (ea + eb + ec)
    acc = jnp.dot(a.astype(BF16), wout_ref[:ATTN_WIDTH, :], preferred_element_type=F32)

    ext_ref[:POOL_HALO, :] = jnp.where(ti > 0, pp_ref[...], 0.0)
    ext_ref[POOL_HALO:POOL_HALO + tm, :] = pc_ref[...]
    ext_ref[POOL_HALO + tm:, :] = jnp.where(ti < seq // tm - 1, pn_ref[...], 0.0)
    pos = ti * tm + lax.broadcasted_iota(jnp.int32, (tm, 1), 0)
    for g, win in enumerate(POOL_WINDOWS):
        cols = slice(g * LANES, (g + 1) * LANES)
        before, after = win // 2, win - win // 2
        tot = ext_ref[POOL_HALO - before:POOL_HALO - before + tm, cols]
        for d in range(-before + 1, after):
            tot = tot + ext_ref[POOL_HALO + d:POOL_HALO + d + tm, cols]
        cnt = (jnp.minimum(pos + after, seq) - jnp.maximum(pos - before, 0)).astype(F32)
        diff = tot / cnt - pc_ref[:, cols]
        mixed = jnp.dot(diff.astype(BF16), pw_ref[g], preferred_element_type=F32) * ps_ref[:, cols]
        acc = acc + jnp.dot(mixed.astype(BF16),
                            wout_ref[ATTN_WIDTH + g * LANES:ATTN_WIDTH + (g + 1) * LANES, :],
                            preferred_element_type=F32)
    out_ref[...] = x_ref[...] + acc


def _mix0(outs, lses, p_in, x2, w_out, pool_w, pool_scale, seq, tm):
    T, D = x2.shape
    pw = p_in.shape[1]
    hb = tm // POOL_HALO
    last = T // POOL_HALO - 1
    row = lambda w: pl.BlockSpec((tm, w), lambda i: (i, 0))
    full = lambda s: pl.BlockSpec(s, lambda i: (0,) * len(s))
    return pl.pallas_call(
        functools.partial(_mix0_kernel, seq=seq, tm=tm),
        grid=(T // tm,),
        in_specs=[row(ATTN_WIDTH)] * 6 + [
            row(pw),
            pl.BlockSpec((POOL_HALO, pw), lambda i: (jnp.maximum(i * hb - 1, 0), 0)),
            pl.BlockSpec((POOL_HALO, pw), lambda i: (jnp.minimum((i + 1) * hb, last), 0)),
            row(D), full(w_out.shape), full(pool_w.shape), full((1, pw))],
        out_specs=row(D),
        out_shape=jax.ShapeDtypeStruct((T, D), F32),
        scratch_shapes=[pltpu.VMEM((tm + 2 * POOL_HALO, pw), F32)],
        compiler_params=_params(("parallel",)),
    )(*outs, *lses, p_in, p_in, p_in, x2, w_out.astype(BF16), pool_w.astype(BF16),
      pool_scale.reshape(1, pw))


def _sgu_kernel(x_ref, g_ref, win_ref, cn_ref, ws_ref, bs_ref, wout_ref, out_ref, gated_ref, *, tm):
    x = x_ref[...]
    width = wout_ref.shape[0]
    h = _rms(x, g_ref[...]).astype(BF16)
    u = _gelu(jnp.dot(h, win_ref[:, :width], preferred_element_type=F32))
    v = _gelu(jnp.dot(h, win_ref[:, width:], preferred_element_type=F32))
    vb = _rms(v, cn_ref[...]).astype(BF16)
    for n in range(tm // SGU_CHUNK):
        rows = slice(n * SGU_CHUNK, (n + 1) * SGU_CHUNK)
        for g in range(SGU_GROUPS):
            cols = slice(g * LANES, (g + 1) * LANES)
            mixed = jnp.dot(ws_ref[g], vb[rows, cols], preferred_element_type=F32) + bs_ref[:, cols]
            gated_ref[rows, cols] = (u[rows, cols] * mixed).astype(BF16)
    out_ref[...] = x + jnp.dot(gated_ref[...], wout_ref[...], preferred_element_type=F32)


def _sgu(x2, g, w_in, c_norm, w_s, b_s, w_out, tm):
    T, D = x2.shape
    width = w_out.shape[0]
    bias = jnp.repeat(b_s.T, width // SGU_GROUPS, axis=1)
    full = lambda s: pl.BlockSpec(s, lambda i: (0,) * len(s))
    return pl.pallas_call(
        functools.partial(_sgu_kernel, tm=tm),
        grid=(T // tm,),
        in_specs=[pl.BlockSpec((tm, D), lambda i: (i, 0)), full((1, D)), full(w_in.shape),
                  full((1, width)), full(w_s.shape), full(bias.shape), full(w_out.shape)],
        out_specs=pl.BlockSpec((tm, D), lambda i: (i, 0)),
        out_shape=jax.ShapeDtypeStruct((T, D), F32),
        scratch_shapes=[pltpu.VMEM((tm, width), BF16)],
        compiler_params=_params(("parallel",)),
    )(x2, g.reshape(1, D), w_in.astype(BF16), c_norm.reshape(1, width), w_s.astype(BF16), bias,
      w_out.astype(BF16))


def _cmpx(v, i, l, descending):
    hi, lo = jnp.maximum(v[i], v[l]), jnp.minimum(v[i], v[l])
    v[i], v[l] = (hi, lo) if descending else (lo, hi)


def _bitonic_sort_desc(v):
    v = list(v)
    n = len(v)
    k = 2
    while k <= n:
        j = k // 2
        while j >= 1:
            for i in range(n):
                l = i ^ j
                if l > i:
                    _cmpx(v, i, l, (i & k) == 0)
            j //= 2
        k *= 2
    return v


def _merge_top(a, b):
    n = len(a)
    v = [jnp.maximum(a[i], b[n - 1 - i]) for i in range(n)]
    j = n // 2
    while j >= 1:
        for i in range(n):
            if (i & j) == 0:
                _cmpx(v, i, i + j, True)
        j //= 2
    return v


def _top_sorted(vals, k):
    groups = [_bitonic_sort_desc(vals[i:i + k]) for i in range(0, len(vals), k)]
    while len(groups) > 1:
        groups = [_merge_top(groups[i], groups[i + 1]) for i in range(0, len(groups), 2)]
    return groups[0]


def _count(preds):
    tot = jnp.where(preds[0], 1.0, 0.0)
    for p in preds[1:]:
        tot = tot + jnp.where(p, 1.0, 0.0)
    return tot


def _twin_bf16(v):
    bits = pltpu.bitcast(v.astype(BF16).astype(F32), U32)
    return bits | (bits >> 16)


def _peer_gate_kernel(x_ref, g_ref, wq_ref, keys_ref, ht_ref, r2_ref, e2_ref, ne_ref,
                      km_ref, om_ref, *, tg):
    K = PEER_TOPK
    nk = PEER_N_KEYS
    chunks = tg // LANES

    @pl.when(pl.program_id(1) == 0)
    def _():
        ht_ref[...] = _as_words(_rms(x_ref[...], g_ref[...]).T)

    qt = jnp.dot(wq_ref[...], _as_bf16(ht_ref[...]), preferred_element_type=F32)
    half_dim = qt.shape[0] // 2
    for side in range(2):
        sc = jnp.dot(keys_ref[side], qt[side * half_dim:(side + 1) * half_dim].astype(BF16),
                     preferred_element_type=F32)
        for c in range(chunks):
            km_ref[side, pl.ds(c, nk, stride=chunks), :] = sc[:, c * LANES:(c + 1) * LANES]

    def keyrows(k):
        return slice(k * chunks, (k + 1) * chunks)

    s1 = [km_ref[0, keyrows(k), :] for k in range(nk)]
    s2 = [km_ref[1, keyrows(k), :] for k in range(nk)]
    A = _top_sorted(s1, K)
    B = _top_sorted(s2, K)

    reach = [K // (i + 1) for i in range(K)]
    cand = [[A[i] + B[j] for j in range(reach[i])] for i in range(K)]
    ninf = jnp.full_like(A[0], -jnp.inf)
    rest = [c for row in cand[1:] for c in row]
    rest = rest + [ninf] * (-len(rest) % K)
    top = cand[0]
    for i in range(0, len(rest), K):
        top = _merge_top(top, _bitonic_sort_desc(rest[i:i + K]))
    thr = top[K - 1]

    gt = [_count([c > thr for c in row]) for row in cand]
    eq = [_count([c == thr for c in row]) for row in cand]
    total_gt = functools.reduce(lambda a, b: a + b, gt)
    room = float(K) - total_gt
    n = []
    for i in range(K):
        n.append(gt[i] + jnp.clip(room, 0.0, eq[i]))
        room = room - eq[i]

    e1 = [jnp.exp(a - A[0]) for a in A]
    e2 = [jnp.exp(b - B[0]) for b in B]
    z = jnp.zeros_like(A[0])
    for i in range(K):
        inner = jnp.zeros_like(z)
        for j in range(reach[i]):
            inner = inner + jnp.where(n[i] > float(j), e2[j], 0.0)
        z = z + e1[i] * inner
    inv_z = 1.0 / z

    pinf = jnp.full_like(A[0], jnp.inf)
    t = []
    for m in range(1, K + 1):
        tm_ = pinf
        for i in range(K):
            tm_ = jnp.minimum(tm_, jnp.where(n[i] >= float(m), A[i], pinf))
        t.append(tm_)

    for k in range(nk):
        om_ref[0, keyrows(k), :] = _count([s1[k] >= tv for tv in t])
        om_ref[1, keyrows(k), :] = jnp.exp(s1[k] - A[0]) * inv_z
        om_ref[2, keyrows(k), :] = _count([b > s2[k] for b in B])
        om_ref[3, keyrows(k), :] = jnp.exp(s2[k] - B[0])

    for c in range(chunks):
        cols = slice(c * LANES, (c + 1) * LANES)
        ne_ref[0, c] = _twin_bf16(om_ref[0, pl.ds(c, nk, stride=chunks), :])
        ne_ref[1, c] = _twin_bf16(om_ref[1, pl.ds(c, nk, stride=chunks), :])
        r2_ref[:, cols] = _as_words(om_ref[2, pl.ds(c, nk, stride=chunks), :])
        e2_ref[:, cols] = _as_words(om_ref[3, pl.ds(c, nk, stride=chunks), :])


def _peer_gates(x2, g, w_q, sub_keys):
    T, D = x2.shape
    tg = VREG_TOKENS
    heads, _, nk, half_dim = sub_keys.shape
    wq_t = w_q.T.astype(BF16)
    return pl.pallas_call(
        functools.partial(_peer_gate_kernel, tg=tg),
        grid=(T // tg, heads),
        in_specs=[pl.BlockSpec((tg, D), lambda i, h: (i, 0)),
                  pl.BlockSpec((1, D), lambda i, h: (0, 0)),
                  pl.BlockSpec((2 * half_dim, D), lambda i, h: (h, 0)),
                  pl.BlockSpec((None, 2, nk, half_dim), lambda i, h: (h, 0, 0, 0))],
        out_specs=[pl.BlockSpec((D // 2, tg), lambda i, h: (0, i)),
                   pl.BlockSpec((nk // 2, tg), lambda i, h: (h, i)),
                   pl.BlockSpec((nk // 2, tg), lambda i, h: (h, i)),
                   pl.BlockSpec((None, 2, tg // LANES, nk, LANES), lambda i, h: (h, 0, i, 0, 0))],
        out_shape=[jax.ShapeDtypeStruct((D // 2, T), U32),
                   jax.ShapeDtypeStruct((heads * nk // 2, T), U32),
                   jax.ShapeDtypeStruct((heads * nk // 2, T), U32),
                   jax.ShapeDtypeStruct((heads, 2, T // LANES, nk, LANES), U32)],
        scratch_shapes=[pltpu.VMEM((2, nk * SUBLANES, LANES), F32),
                        pltpu.VMEM((4, nk * SUBLANES, LANES), F32)],
        compiler_params=_params(("parallel", "arbitrary")),
    )(x2, g.reshape(1, D), wq_t, sub_keys.astype(BF16))


KEY_BLOCK = 2
ROW_SPLIT = 4


def _gate_hidden(hid_ref, ga_ref, r2_ref, e2_ref, ne_ref, first_key, c, al0):
    nk = PEER_N_KEYS
    tiles = range(0, nk, BF16_ROWS)
    cols = slice(c * LANES, (c + 1) * LANES)

    def row_tile(h, q, al):
        return _as_bf16(ne_ref[h, q, c, pl.ds(first_key + al, SUBLANES, stride=0), :])

    gates = [[None for _ in tiles] for _ in range(KEY_BLOCK)]
    for h in range(PEER_HEADS):
        counts = [row_tile(h, 0, al0 + k) for k in range(KEY_BLOCK)]
        factors = [row_tile(h, 1, al0 + k) for k in range(KEY_BLOCK)]
        for ti, r0 in enumerate(tiles):
            hrows = slice((h * nk + r0) // 2, (h * nk + r0) // 2 + SUBLANES)
            rank, e2 = _as_bf16(r2_ref[hrows, cols]), _as_bf16(e2_ref[hrows, cols])
            for k in range(KEY_BLOCK):
                term = jnp.where(rank < counts[k], e2, jnp.zeros_like(e2)) * factors[k]
                gates[k][ti] = term if gates[k][ti] is None else gates[k][ti] + term
    for k in range(KEY_BLOCK):
        for ti, r0 in enumerate(tiles):
            rows = slice((al0 + k) * nk + r0, (al0 + k) * nk + r0 + BF16_ROWS)
            ga_ref[rows, cols] = gates[k][ti] * _gelu(hid_ref[rows, cols].astype(BF16))


def _peer_expert_kernel(ht_ref, wd_ref, wu_ref, r2_ref, e2_ref, ne_ref, x_ref, gf_ref, out_ref,
                        acc_ref, hid0_ref, hid1_ref, ga0_ref, ga1_ref, *, final_norm, n_blocks, n_work):
    s = pl.program_id(0)
    eb = hid0_ref.shape[0]
    gate_item = jnp.clip(s - 1, 0, n_work - 1)
    up_block = jnp.clip(s - 2, 0, n_work - 1) % n_blocks
    first_key = (gate_item % n_blocks) * (eb // PEER_N_KEYS)

    @pl.when(s == 0)
    def _():
        hid1_ref[...] = jnp.zeros_like(hid1_ref)
        ga0_ref[...] = jnp.zeros_like(ga0_ref)

    @pl.when(up_block == 0)
    def _():
        acc_ref[...] = jnp.zeros_like(acc_ref)

    def stages(hid_new, hid_old, ga_new, ga_old):
        tm = hid_new.shape[1]
        d_model = acc_ref.shape[0]

        def up(n, m):
            cols = slice(n * MXU_WIDTH, (n + 1) * MXU_WIDTH)
            rows = slice(m * (d_model // ROW_SPLIT) // 2, (m + 1) * (d_model // ROW_SPLIT) // 2)
            out_rows = slice(m * (d_model // ROW_SPLIT), (m + 1) * (d_model // ROW_SPLIT))
            acc_ref[out_rows, cols] += jnp.dot(_as_bf16(wu_ref[rows, :]), ga_old[:, cols],
                                               preferred_element_type=F32)

        def down(n, m):
            cols = slice(n * MXU_WIDTH, (n + 1) * MXU_WIDTH)
            rows = slice(m * (eb // ROW_SPLIT) // 2, (m + 1) * (eb // ROW_SPLIT) // 2)
            out_rows = slice(m * (eb // ROW_SPLIT), (m + 1) * (eb // ROW_SPLIT))
            hid_new[out_rows, cols] = jnp.dot(_as_bf16(wd_ref[rows, :]), _as_bf16(ht_ref[:, cols]),
                                              preferred_element_type=F32)

        mxu = [functools.partial(f, n, m) for n in range(tm // MXU_WIDTH) for f in (up, down)
               for m in range(ROW_SPLIT)]
        vpu = [functools.partial(_gate_hidden, hid_old, ga_new, r2_ref, e2_ref, ne_ref, first_key, c, al0)
               for c in range(tm // LANES) for al0 in range(0, eb // PEER_N_KEYS, KEY_BLOCK)]
        per = len(vpu) // len(mxu)
        assert per * len(mxu) == len(vpu)
        for i, piece in enumerate(mxu):
            piece()
            for g in vpu[i * per:(i + 1) * per]:
                g()

    pl.when(s % 2 == 0)(lambda: stages(hid0_ref, hid1_ref, ga1_ref, ga0_ref))
    pl.when(s % 2 == 1)(lambda: stages(hid1_ref, hid0_ref, ga0_ref, ga1_ref))

    @pl.when((s >= 2) & (up_block == n_blocks - 1))
    def _():
        y = x_ref[...] + acc_ref[...].T
        out_ref[...] = _rms(y, gf_ref[...]) if final_norm else y


def _peer_experts(ht, r2, e2, ne, x2, w_down, w_up, g_final, final_norm, tm, eb):
    T, D = x2.shape
    n_exp = w_down.shape[0]
    heads, _, _, nk, _ = ne.shape
    n_blocks = n_exp // eb
    n_work = (T // tm) * n_blocks

    def item(lag):
        def split(s):
            w = jnp.clip(s - lag, 0, n_work - 1)
            return w // n_blocks, w % n_blocks
        return split

    down, gate, up = item(0), item(1), item(2)
    return pl.pallas_call(
        functools.partial(_peer_expert_kernel, final_norm=final_norm, n_blocks=n_blocks, n_work=n_work),
        grid=(n_work + 2,),
        in_specs=[pl.BlockSpec((D // 2, tm), lambda s: (0, down(s)[0])),
                  pl.BlockSpec((eb // 2, D), lambda s: (down(s)[1], 0)),
                  pl.BlockSpec((D // 2, eb), lambda s: (0, up(s)[1])),
                  pl.BlockSpec((heads * nk // 2, tm), lambda s: (0, gate(s)[0])),
                  pl.BlockSpec((heads * nk // 2, tm), lambda s: (0, gate(s)[0])),
                  pl.BlockSpec((heads, 2, tm // LANES, nk, LANES), lambda s: (0, 0, gate(s)[0], 0, 0)),
                  pl.BlockSpec((tm, D), lambda s: (up(s)[0], 0)),
                  pl.BlockSpec((1, D), lambda s: (0, 0))],
        out_specs=pl.BlockSpec((tm, D), lambda s: (up(s)[0], 0)),
        out_shape=jax.ShapeDtypeStruct((T, D), F32),
        scratch_shapes=[pltpu.VMEM((D, tm), F32), pltpu.VMEM((eb, tm), F32), pltpu.VMEM((eb, tm), F32),
                        pltpu.VMEM((eb, tm), BF16), pltpu.VMEM((eb, tm), BF16)],
        compiler_params=_params(("arbitrary",)),
    )(ht, _pack_rows(w_down), _pack_rows(w_up.T), r2, e2, ne, x2, g_final.reshape(1, D))


def _peer(x2, g, w_q, sub_keys, w_down, w_up, g_final, final_norm, tm, eb):
    ht, r2, e2, ne = _peer_gates(x2, g, w_q, sub_keys)
    return _peer_experts(ht, r2, e2, ne, x2, w_down, w_up, g_final, final_norm, tm, eb)


def kernel(x, norm_mix, norm_ffn, norm_final, ab_w_in, ab_w_out, pool_w, pool_scale, c_w_in,
           c_norm, c_w_s, c_b_s, c_w_out, peer_w_q, peer_sub_keys, peer_w_down, peer_w_up):
    B, S, D = x.shape
    depth = norm_mix.shape[0]
    tm = 512
    eb = 1024
    assert S % tm == 0 and (B * S) % VREG_TOKENS == 0
    assert all(S % (d * QUERY_BLOCK) == 0 for d in DILATIONS)
    x2 = x.reshape(B * S, D)
    for layer in range(depth):
        j = layer // 2
        if layer % 2 == 0:
            qkv, p_in = _inproj(x2, norm_mix[layer], ab_w_in[j], S, tm)
            outs, lses = zip(*[_attention(qkv, B, S, dil) for dil in DILATIONS])
            x2 = _mix0(outs, lses, p_in, x2, ab_w_out[j], pool_w[j], pool_scale[j], S, tm)
        else:
            x2 = _sgu(x2, norm_mix[layer], c_w_in[j], c_norm[j], c_w_s[j], c_b_s[j], c_w_out[j], tm)
        x2 = _peer(x2, norm_ffn[layer], peer_w_q[layer], peer_sub_keys[layer], peer_w_down[layer],
                   peer_w_up[layer], norm_final, layer == depth - 1, tm, eb)
    return x2.reshape(B, S, D)
```

```python
import functools
import math

import jax
import jax.numpy as jnp
import numpy as np
from jax import lax
from jax.experimental import pallas as pl
from jax.experimental.pallas import tpu as pltpu

F32 = jnp.float32
BF16 = jnp.bfloat16

LANES = 128
SUBLANES = 8
BF16_ROWS = 2 * SUBLANES
MXU_WIDTH = 256
VREG_TOKENS = SUBLANES * LANES
VMEM_LIMIT = 56 * 1024 * 1024

ATTN_HEADS = 8
HEAD_DIM = 64
ATTN_WIDTH = ATTN_HEADS * HEAD_DIM
DILATIONS = (1, 4, 16)
ATTN_HALF = 64
QUERY_BLOCK = 128
ROPE_THETA = 500000.0
ROPE_DIM = HEAD_DIM // 4
POOL_WINDOWS = (2, 4, 8, 16)
POOL_HALO = 8
SGU_CHUNK = 128
SGU_GROUPS = 8
PEER_HEADS = 8
PEER_N_KEYS = 128
PEER_TOPK = 16
RMS_EPS = 1e-6
NEG_INF = -1e30
GELU_C = math.sqrt(2.0 / math.pi)


def _params(semantics):
    return pltpu.CompilerParams(dimension_semantics=semantics, vmem_limit_bytes=VMEM_LIMIT)


def _rms(x, g):
    return x * lax.rsqrt(jnp.mean(x * x, axis=-1, keepdims=True) + RMS_EPS) * g


U32 = jnp.uint32


def _as_bf16(words):
    return pltpu.bitcast(words, BF16)


def _as_words(x):
    return pltpu.bitcast(x.astype(BF16), U32)


def _pack_kernel(w_ref, o_ref, *, transpose):
    w = w_ref[...]
    o_ref[...] = _as_words(w.T if transpose else w)


def _pack_rows(w, transpose, rows=512):
    N, R, C = w.shape
    out_block, out_map = ((None, C // 2, rows), lambda n, i: (n, 0, i)) if transpose else (
        (None, rows // 2, C), lambda n, i: (n, i, 0))
    out_shape = (N, C // 2, R) if transpose else (N, R // 2, C)
    return pl.pallas_call(
        functools.partial(_pack_kernel, transpose=transpose),
        grid=(N, R // rows),
        in_specs=[pl.BlockSpec((None, rows, C), lambda n, i: (n, i, 0))],
        out_specs=pl.BlockSpec(out_block, out_map),
        out_shape=jax.ShapeDtypeStruct(out_shape, U32),
        compiler_params=_params(("parallel", "parallel")),
    )(w)


def _gelu(x):
    return x * (0.5 * (1.0 + jnp.tanh(GELU_C * (x + 0.044715 * (x * x * x)))))


def _inproj_kernel(x_ref, g_ref, w_ref, cos_ref, sa_ref, sb_ref, qkv_ref, p_ref):
    h = _rms(x_ref[...], g_ref[...]).astype(BF16)
    proj = jnp.dot(h, w_ref[...], preferred_element_type=F32)
    c, sa, sb = cos_ref[...], sa_ref[...], sb_ref[...]
    half = ROPE_DIM // 2
    for col in range(0, 2 * ATTN_WIDTH, LANES):
        t = proj[:, col:col + LANES]
        up = pltpu.roll(t, LANES - half, axis=1)
        dn = pltpu.roll(t, half, axis=1)
        qkv_ref[:, col:col + LANES] = (t * c + up * sa + dn * sb).astype(BF16)
    qkv_ref[:, 2 * ATTN_WIDTH:] = proj[:, 2 * ATTN_WIDTH:3 * ATTN_WIDTH].astype(BF16)
    p_ref[...] = proj[:, 3 * ATTN_WIDTH:]


def _rope_tables(seq):
    half = ROPE_DIM // 2
    pos = jnp.arange(seq, dtype=F32)
    inv_freq = ROPE_THETA ** (-jnp.arange(0, ROPE_DIM, 2, dtype=F32) / ROPE_DIM)
    ang = pos[:, None] * inv_freq[None, :]
    cos, sin = jnp.cos(ang), jnp.sin(ang)
    pad = HEAD_DIM - ROPE_DIM
    ones = jnp.ones((seq, pad), F32)
    zeros = jnp.zeros((seq, pad), F32)
    zh = jnp.zeros((seq, half), F32)
    c = jnp.concatenate([cos, cos, ones], axis=1)
    sa = jnp.concatenate([-sin, zh, zeros], axis=1)
    sb = jnp.concatenate([zh, sin, zeros], axis=1)
    return tuple(jnp.concatenate([t, t], axis=1) for t in (c, sa, sb))


def _inproj(x2, g, w_in, seq, tm):
    T, D = x2.shape
    n_out = w_in.shape[1]
    pool_w = n_out - 3 * ATTN_WIDTH
    c, sa, sb = _rope_tables(seq)
    spt = seq // tm
    tab = pl.BlockSpec((tm, LANES), lambda i: (i % spt, 0))
    return pl.pallas_call(
        _inproj_kernel,
        grid=(T // tm,),
        in_specs=[pl.BlockSpec((tm, D), lambda i: (i, 0)),
                  pl.BlockSpec((1, D), lambda i: (0, 0)),
                  pl.BlockSpec((D, n_out), lambda i: (0, 0)),
                  tab, tab, tab],
        out_specs=[pl.BlockSpec((tm, 3 * ATTN_WIDTH), lambda i: (i, 0)),
                   pl.BlockSpec((tm, pool_w), lambda i: (i, 0))],
        out_shape=[jax.ShapeDtypeStruct((T, 3 * ATTN_WIDTH), BF16),
                   jax.ShapeDtypeStruct((T, pool_w), F32)],
        compiler_params=_params(("parallel",)),
    )(x2, g.reshape(1, D), w_in.astype(BF16), c, sa, sb)


def _attn_kernel(q_ref, k_ref, v_ref, o_ref, lse_ref, *, length):
    kw = min(QUERY_BLOCK + 2 * ATTN_HALF, length)
    scale = HEAD_DIM ** -0.5
    lane = lax.broadcasted_iota(jnp.int32, (QUERY_BLOCK, LANES), 1)
    first_head = lane < HEAD_DIM

    def block(qb, carry):
        q0 = pl.multiple_of(qb * QUERY_BLOCK, QUERY_BLOCK)
        start = pl.multiple_of(jnp.clip(q0 - ATTN_HALF, 0, length - kw), ATTN_HALF)
        qpos = q0 + lax.broadcasted_iota(jnp.int32, (QUERY_BLOCK, kw), 0)
        kpos = start + lax.broadcasted_iota(jnp.int32, (QUERY_BLOCK, kw), 1)
        valid = jnp.abs(kpos - qpos) <= ATTN_HALF
        for col in range(0, ATTN_WIDTH, LANES):
            qp = q_ref[pl.ds(q0, QUERY_BLOCK), col:col + LANES]
            kp = k_ref[pl.ds(start, kw), col:col + LANES]
            vp = v_ref[pl.ds(start, kw), col:col + LANES]
            outs, lses = [], []
            for mask in (first_head, jnp.logical_not(first_head)):
                qm = jnp.where(mask, qp, jnp.zeros_like(qp))
                s = lax.dot_general(qm, kp, (((1,), (1,)), ((), ())),
                                    preferred_element_type=F32) * scale
                s = jnp.where(valid, s, NEG_INF)
                m = jnp.max(s, axis=-1, keepdims=True)
                p = jnp.exp(s - m)
                l = jnp.sum(p, axis=-1, keepdims=True)
                outs.append(jnp.dot(p.astype(BF16), vp, preferred_element_type=F32) / l)
                lses.append(jnp.broadcast_to(m + jnp.log(l), (QUERY_BLOCK, LANES)))
            o_ref[pl.ds(q0, QUERY_BLOCK), col:col + LANES] = (
                jnp.where(first_head, outs[0], outs[1]).astype(BF16))
            lse_ref[pl.ds(q0, QUERY_BLOCK), col:col + LANES] = (
                jnp.where(first_head, lses[0], lses[1]))
        return carry

    lax.fori_loop(0, length // QUERY_BLOCK, block, 0)


def _attention(qkv, batch, seq, dil):
    length = seq // dil
    width = 3 * ATTN_WIDTH
    view = qkv.reshape(batch, length, dil * width)

    def part(j):
        return pl.BlockSpec((None, length, ATTN_WIDTH), lambda b, r: (b, 0, 3 * r + j))

    out = pl.BlockSpec((None, length, ATTN_WIDTH), lambda b, r: (b, 0, r))
    o, lse = pl.pallas_call(
        functools.partial(_attn_kernel, length=length),
        grid=(batch, dil),
        in_specs=[part(0), part(1), part(2)],
        out_specs=[out, out],
        out_shape=[jax.ShapeDtypeStruct((batch, length, dil * ATTN_WIDTH), BF16),
                   jax.ShapeDtypeStruct((batch, length, dil * ATTN_WIDTH), F32)],
        compiler_params=_params(("parallel", "parallel")),
    )(view, view, view)
    return o.reshape(batch * seq, ATTN_WIDTH), lse.reshape(batch * seq, ATTN_WIDTH)


def _mix0_kernel(o1, o2, o3, l1, l2, l3, pc_ref, pp_ref, pn_ref, x_ref, wout_ref, pw_ref, ps_ref,
                 out_ref, ext_ref, *, seq, tm):
    ti = pl.program_id(0) % (seq // tm)
    la, lb, lc = l1[...], l2[...], l3[...]
    m = jnp.maximum(jnp.maximum(la, lb), lc)
    ea, eb, ec = jnp.exp(la - m), jnp.exp(lb - m), jnp.exp(lc - m)
    a = (o1[...].astype(F32) * ea + o2[...].astype(F32) * eb + o3[...].astype(F32) * ec) / (ea + eb + ec)
    acc = jnp.dot(a.astype(BF16), wout_ref[:ATTN_WIDTH, :], preferred_element_type=F32)

    ext_ref[:POOL_HALO, :] = jnp.where(ti > 0, pp_ref[...], 0.0)
    ext_ref[POOL_HALO:POOL_HALO + tm, :] = pc_ref[...]
    ext_ref[POOL_HALO + tm:, :] = jnp.where(ti < seq // tm - 1, pn_ref[...], 0.0)
    pos = ti * tm + lax.broadcasted_iota(jnp.int32, (tm, 1), 0)
    for g, win in enumerate(POOL_WINDOWS):
        cols = slice(g * LANES, (g + 1) * LANES)
        before, after = win // 2, win - win // 2
        tot = ext_ref[POOL_HALO - before:POOL_HALO - before + tm, cols]
        for d in range(-before + 1, after):
            tot = tot + ext_ref[POOL_HALO + d:POOL_HALO + d + tm, cols]
        cnt = (jnp.minimum(pos + after, seq) - jnp.maximum(pos - before, 0)).astype(F32)
        diff = tot / cnt - pc_ref[:, cols]
        mixed = jnp.dot(diff.astype(BF16), pw_ref[g], preferred_element_type=F32) * ps_ref[:, cols]
        acc = acc + jnp.dot(mixed.astype(BF16),
                            wout_ref[ATTN_WIDTH + g * LANES:ATTN_WIDTH + (g + 1) * LANES, :],
                            preferred_element_type=F32)
    out_ref[...] = x_ref[...] + acc


def _mix0(outs, lses, p_in, x2, w_out, pool_w, pool_scale, seq, tm):
    T, D = x2.shape
    pw = p_in.shape[1]
    hb = tm // POOL_HALO
    last = T // POOL_HALO - 1
    row = lambda w: pl.BlockSpec((tm, w), lambda i: (i, 0))
    full = lambda s: pl.BlockSpec(s, lambda i: (0,) * len(s))
    return pl.pallas_call(
        functools.partial(_mix0_kernel, seq=seq, tm=tm),
        grid=(T // tm,),
        in_specs=[row(ATTN_WIDTH)] * 6 + [
            row(pw),
            pl.BlockSpec((POOL_HALO, pw), lambda i: (jnp.maximum(i * hb - 1, 0), 0)),
            pl.BlockSpec((POOL_HALO, pw), lambda i: (jnp.minimum((i + 1) * hb, last), 0)),
            row(D), full(w_out.shape), full(pool_w.shape), full((1, pw))],
        out_specs=row(D),
        out_shape=jax.ShapeDtypeStruct((T, D), F32),
        scratch_shapes=[pltpu.VMEM((tm + 2 * POOL_HALO, pw), F32)],
        compiler_params=_params(("parallel",)),
    )(*outs, *lses, p_in, p_in, p_in, x2, w_out.astype(BF16), pool_w.astype(BF16),
      pool_scale.reshape(1, pw))


def _sgu_kernel(x_ref, g_ref, win_ref, cn_ref, ws_ref, bs_ref, wout_ref, out_ref, gated_ref, *, tm):
    x = x_ref[...]
    width = wout_ref.shape[0]
    h = _rms(x, g_ref[...]).astype(BF16)
    u = _gelu(jnp.dot(h, win_ref[:, :width], preferred_element_type=F32))
    v = _gelu(jnp.dot(h, win_ref[:, width:], preferred_element_type=F32))
    vb = _rms(v, cn_ref[...]).astype(BF16)
    for n in range(tm // SGU_CHUNK):
        rows = slice(n * SGU_CHUNK, (n + 1) * SGU_CHUNK)
        for g in range(SGU_GROUPS):
            cols = slice(g * LANES, (g + 1) * LANES)
            mixed = jnp.dot(ws_ref[g], vb[rows, cols], preferred_element_type=F32) + bs_ref[:, cols]
            gated_ref[rows, cols] = (u[rows, cols] * mixed).astype(BF16)
    out_ref[...] = x + jnp.dot(gated_ref[...], wout_ref[...], preferred_element_type=F32)


def _sgu(x2, g, w_in, c_norm, w_s, b_s, w_out, tm):
    T, D = x2.shape
    width = w_out.shape[0]
    bias = jnp.repeat(b_s.T, width // SGU_GROUPS, axis=1)
    full = lambda s: pl.BlockSpec(s, lambda i: (0,) * len(s))
    return pl.pallas_call(
        functools.partial(_sgu_kernel, tm=tm),
        grid=(T // tm,),
        in_specs=[pl.BlockSpec((tm, D), lambda i: (i, 0)), full((1, D)), full(w_in.shape),
                  full((1, width)), full(w_s.shape), full(bias.shape), full(w_out.shape)],
        out_specs=pl.BlockSpec((tm, D), lambda i: (i, 0)),
        out_shape=jax.ShapeDtypeStruct((T, D), F32),
        scratch_shapes=[pltpu.VMEM((tm, width), BF16)],
        compiler_params=_params(("parallel",)),
    )(x2, g.reshape(1, D), w_in.astype(BF16), c_norm.reshape(1, width), w_s.astype(BF16), bias,
      w_out.astype(BF16))


def _cmpx(v, i, l, descending):
    hi, lo = jnp.maximum(v[i], v[l]), jnp.minimum(v[i], v[l])
    v[i], v[l] = (hi, lo) if descending else (lo, hi)


def _bitonic_sort_desc(v):
    v = list(v)
    n = len(v)
    k = 2
    while k <= n:
        j = k // 2
        while j >= 1:
            for i in range(n):
                l = i ^ j
                if l > i:
                    _cmpx(v, i, l, (i & k) == 0)
            j //= 2
        k *= 2
    return v


def _merge_top(a, b):
    n = len(a)
    v = [jnp.maximum(a[i], b[n - 1 - i]) for i in range(n)]
    j = n // 2
    while j >= 1:
        for i in range(n):
            if (i & j) == 0:
                _cmpx(v, i, i + j, True)
        j //= 2
    return v


def _top_sorted(vals, k):
    groups = [_bitonic_sort_desc(vals[i:i + k]) for i in range(0, len(vals), k)]
    while len(groups) > 1:
        groups = [_merge_top(groups[i], groups[i + 1]) for i in range(0, len(groups), 2)]
    return groups[0]


def _count(preds):
    tot = jnp.where(preds[0], 1.0, 0.0)
    for p in preds[1:]:
        tot = tot + jnp.where(p, 1.0, 0.0)
    return tot


def _twin_bf16(v):
    bits = pltpu.bitcast(v.astype(BF16).astype(F32), U32)
    return bits | (bits >> 16)


def _peer_gate_kernel(x_ref, g_ref, wq_ref, keys_ref, ht_ref, r2_ref, e2_ref, ne_ref,
                      km_ref, om_ref, *, tg):
    K = PEER_TOPK
    nk = PEER_N_KEYS
    chunks = tg // LANES

    @pl.when(pl.program_id(1) == 0)
    def _():
        ht_ref[...] = _as_words(_rms(x_ref[...], g_ref[...]).T)

    qt = jnp.dot(wq_ref[...], _as_bf16(ht_ref[...]), preferred_element_type=F32)
    half_dim = qt.shape[0] // 2
    for side in range(2):
        sc = jnp.dot(keys_ref[side], qt[side * half_dim:(side + 1) * half_dim].astype(BF16),
                     preferred_element_type=F32)
        for c in range(chunks):
            km_ref[side, pl.ds(c, nk, stride=chunks), :] = sc[:, c * LANES:(c + 1) * LANES]

    def keyrows(k):
        return slice(k * chunks, (k + 1) * chunks)

    s1 = [km_ref[0, keyrows(k), :] for k in range(nk)]
    s2 = [km_ref[1, keyrows(k), :] for k in range(nk)]
    A = _top_sorted(s1, K)
    B = _top_sorted(s2, K)

    reach = [K // (i + 1) for i in range(K)]
    cand = [[A[i] + B[j] for j in range(reach[i])] for i in range(K)]
    ninf = jnp.full_like(A[0], -jnp.inf)
    rest = [c for row in cand[1:] for c in row]
    rest = rest + [ninf] * (-len(rest) % K)
    top = cand[0]
    for i in range(0, len(rest), K):
        top = _merge_top(top, _bitonic_sort_desc(rest[i:i + K]))
    thr = top[K - 1]

    gt = [_count([c > thr for c in row]) for row in cand]
    eq = [_count([c == thr for c in row]) for row in cand]
    total_gt = functools.reduce(lambda a, b: a + b, gt)
    room = float(K) - total_gt
    n = []
    for i in range(K):
        n.append(gt[i] + jnp.clip(room, 0.0, eq[i]))
        room = room - eq[i]

    e1 = [jnp.exp(a - A[0]) for a in A]
    e2 = [jnp.exp(b - B[0]) for b in B]
    z = jnp.zeros_like(A[0])
    for i in range(K):
        inner = jnp.zeros_like(z)
        for j in range(reach[i]):
            inner = inner + jnp.where(n[i] > float(j), e2[j], 0.0)
        z = z + e1[i] * inner
    inv_z = 1.0 / z

    pinf = jnp.full_like(A[0], jnp.inf)
    t = []
    for m in range(1, K + 1):
        tm_ = pinf
        for i in range(K):
            tm_ = jnp.minimum(tm_, jnp.where(n[i] >= float(m), A[i], pinf))
        t.append(tm_)

    for k in range(nk):
        om_ref[0, keyrows(k), :] = _count([s1[k] >= tv for tv in t])
        om_ref[1, keyrows(k), :] = jnp.exp(s1[k] - A[0]) * inv_z
        om_ref[2, keyrows(k), :] = _count([b > s2[k] for b in B])
        om_ref[3, keyrows(k), :] = jnp.exp(s2[k] - B[0])

    for c in range(chunks):
        cols = slice(c * LANES, (c + 1) * LANES)
        ne_ref[0, c] = _twin_bf16(om_ref[0, pl.ds(c, nk, stride=chunks), :])
        ne_ref[1, c] = _twin_bf16(om_ref[1, pl.ds(c, nk, stride=chunks), :])
        r2_ref[:, cols] = _as_words(om_ref[2, pl.ds(c, nk, stride=chunks), :])
        e2_ref[:, cols] = _as_words(om_ref[3, pl.ds(c, nk, stride=chunks), :])


def _peer_gates(x2, g, w_q, sub_keys):
    T, D = x2.shape
    tg = VREG_TOKENS
    heads, _, nk, half_dim = sub_keys.shape
    wq_t = w_q.T.astype(BF16)
    return pl.pallas_call(
        functools.partial(_peer_gate_kernel, tg=tg),
        grid=(T // tg, heads),
        in_specs=[pl.BlockSpec((tg, D), lambda i, h: (i, 0)),
                  pl.BlockSpec((1, D), lambda i, h: (0, 0)),
                  pl.BlockSpec((2 * half_dim, D), lambda i, h: (h, 0)),
                  pl.BlockSpec((None, 2, nk, half_dim), lambda i, h: (h, 0, 0, 0))],
        out_specs=[pl.BlockSpec((D // 2, tg), lambda i, h: (0, i)),
                   pl.BlockSpec((nk // 2, tg), lambda i, h: (h, i)),
                   pl.BlockSpec((nk // 2, tg), lambda i, h: (h, i)),
                   pl.BlockSpec((None, 2, tg // LANES, nk, LANES), lambda i, h: (h, 0, i, 0, 0))],
        out_shape=[jax.ShapeDtypeStruct((D // 2, T), U32),
                   jax.ShapeDtypeStruct((heads * nk // 2, T), U32),
                   jax.ShapeDtypeStruct((heads * nk // 2, T), U32),
                   jax.ShapeDtypeStruct((heads, 2, T // LANES, nk, LANES), U32)],
        scratch_shapes=[pltpu.VMEM((2, nk * SUBLANES, LANES), F32),
                        pltpu.VMEM((4, nk * SUBLANES, LANES), F32)],
        compiler_params=_params(("parallel", "arbitrary")),
    )(x2, g.reshape(1, D), wq_t, sub_keys.astype(BF16))


KEY_BLOCK = 2
ROW_SPLIT = 4


def _gate_hidden(hid_ref, ga_ref, r2_ref, e2_ref, ne_ref, first_key, c, al0):
    nk = PEER_N_KEYS
    tiles = range(0, nk, BF16_ROWS)
    cols = slice(c * LANES, (c + 1) * LANES)

    def row_tile(h, q, al):
        return _as_bf16(ne_ref[h, q, c, pl.ds(first_key + al, SUBLANES, stride=0), :])

    gates = [[None for _ in tiles] for _ in range(KEY_BLOCK)]
    for h in range(PEER_HEADS):
        counts = [row_tile(h, 0, al0 + k) for k in range(KEY_BLOCK)]
        factors = [row_tile(h, 1, al0 + k) for k in range(KEY_BLOCK)]
        for ti, r0 in enumerate(tiles):
            hrows = slice((h * nk + r0) // 2, (h * nk + r0) // 2 + SUBLANES)
            rank, e2 = _as_bf16(r2_ref[hrows, cols]), _as_bf16(e2_ref[hrows, cols])
            for k in range(KEY_BLOCK):
                term = jnp.where(rank < counts[k], e2, jnp.zeros_like(e2)) * factors[k]
                gates[k][ti] = term if gates[k][ti] is None else gates[k][ti] + term
    for k in range(KEY_BLOCK):
        for ti, r0 in enumerate(tiles):
            rows = slice((al0 + k) * nk + r0, (al0 + k) * nk + r0 + BF16_ROWS)
            ga_ref[rows, cols] = gates[k][ti] * _gelu(hid_ref[rows, cols].astype(BF16))


def _peer_expert_kernel(ht_ref, wd_ref, wu_ref, r2_ref, e2_ref, ne_ref, x_ref, gf_ref, out_ref,
                        acc_ref, hid0_ref, hid1_ref, ga0_ref, ga1_ref, *, final_norm, n_blocks, n_work):
    s = pl.program_id(0)
    eb = hid0_ref.shape[0]
    gate_item = jnp.clip(s - 1, 0, n_work - 1)
    up_block = jnp.clip(s - 2, 0, n_work - 1) % n_blocks
    first_key = (gate_item % n_blocks) * (eb // PEER_N_KEYS)

    @pl.when(s == 0)
    def _():
        hid1_ref[...] = jnp.zeros_like(hid1_ref)
        ga0_ref[...] = jnp.zeros_like(ga0_ref)

    @pl.when(up_block == 0)
    def _():
        acc_ref[...] = jnp.zeros_like(acc_ref)

    def stages(hid_new, hid_old, ga_new, ga_old):
        tm = hid_new.shape[1]
        d_model = acc_ref.shape[0]

        def up(n, m):
            cols = slice(n * MXU_WIDTH, (n + 1) * MXU_WIDTH)
            rows = slice(m * (d_model // ROW_SPLIT) // 2, (m + 1) * (d_model // ROW_SPLIT) // 2)
            out_rows = slice(m * (d_model // ROW_SPLIT), (m + 1) * (d_model // ROW_SPLIT))
            acc_ref[out_rows, cols] += jnp.dot(_as_bf16(wu_ref[rows, :]), ga_old[:, cols],
                                               preferred_element_type=F32)

        def down(n, m):
            cols = slice(n * MXU_WIDTH, (n + 1) * MXU_WIDTH)
            rows = slice(m * (eb // ROW_SPLIT) // 2, (m + 1) * (eb // ROW_SPLIT) // 2)
            out_rows = slice(m * (eb // ROW_SPLIT), (m + 1) * (eb // ROW_SPLIT))
            hid_new[out_rows, cols] = jnp.dot(_as_bf16(wd_ref[rows, :]), _as_bf16(ht_ref[:, cols]),
                                              preferred_element_type=F32)

        mxu = [functools.partial(f, n, m) for n in range(tm // MXU_WIDTH) for f in (up, down)
               for m in range(ROW_SPLIT)]
        vpu = [functools.partial(_gate_hidden, hid_old, ga_new, r2_ref, e2_ref, ne_ref, first_key, c, al0)
               for c in range(tm // LANES) for al0 in range(0, eb // PEER_N_KEYS, KEY_BLOCK)]
        per = len(vpu) // len(mxu)
        assert per * len(mxu) == len(vpu)
        for i, piece in enumerate(mxu):
            piece()
            for g in vpu[i * per:(i + 1) * per]:
                g()

    pl.when(s % 2 == 0)(lambda: stages(hid0_ref, hid1_ref, ga1_ref, ga0_ref))
    pl.when(s % 2 == 1)(lambda: stages(hid1_ref, hid0_ref, ga0_ref, ga1_ref))

    @pl.when((s >= 2) & (up_block == n_blocks - 1))
    def _():
        y = x_ref[...] + acc_ref[...].T
        out_ref[...] = _rms(y, gf_ref[...]) if final_norm else y


def _peer_experts(ht, r2, e2, ne, x2, wd_words, wu_words, layer, g_final, final_norm, tm, eb):
    T, D = x2.shape
    n_exp = wu_words.shape[2]
    heads, _, _, nk, _ = ne.shape
    n_blocks = n_exp // eb
    n_work = (T // tm) * n_blocks

    def item(lag):
        def split(s):
            w = jnp.clip(s - lag, 0, n_work - 1)
            return w // n_blocks, w % n_blocks
        return split

    down, gate, up = item(0), item(1), item(2)
    return pl.pallas_call(
        functools.partial(_peer_expert_kernel, final_norm=final_norm, n_blocks=n_blocks, n_work=n_work),
        grid=(n_work + 2,),
        in_specs=[pl.BlockSpec((D // 2, tm), lambda s: (0, down(s)[0])),
                  pl.BlockSpec((None, eb // 2, D), lambda s: (layer, down(s)[1], 0)),
                  pl.BlockSpec((None, D // 2, eb), lambda s: (layer, 0, up(s)[1])),
                  pl.BlockSpec((heads * nk // 2, tm), lambda s: (0, gate(s)[0])),
                  pl.BlockSpec((heads * nk // 2, tm), lambda s: (0, gate(s)[0])),
                  pl.BlockSpec((heads, 2, tm // LANES, nk, LANES), lambda s: (0, 0, gate(s)[0], 0, 0)),
                  pl.BlockSpec((tm, D), lambda s: (up(s)[0], 0)),
                  pl.BlockSpec((1, D), lambda s: (0, 0))],
        out_specs=pl.BlockSpec((tm, D), lambda s: (up(s)[0], 0)),
        out_shape=jax.ShapeDtypeStruct((T, D), F32),
        scratch_shapes=[pltpu.VMEM((D, tm), F32), pltpu.VMEM((eb, tm), F32), pltpu.VMEM((eb, tm), F32),
                        pltpu.VMEM((eb, tm), BF16), pltpu.VMEM((eb, tm), BF16)],
        compiler_params=_params(("arbitrary",)),
    )(ht, wd_words, wu_words, r2, e2, ne, x2, g_final.reshape(1, D))


def kernel(x, norm_mix, norm_ffn, norm_final, ab_w_in, ab_w_out, pool_w, pool_scale, c_w_in,
           c_norm, c_w_s, c_b_s, c_w_out, peer_w_q, peer_sub_keys, peer_w_down, peer_w_up):
    B, S, D = x.shape
    depth = norm_mix.shape[0]
    tm = 512
    eb = 1024
    assert S % tm == 0 and (B * S) % VREG_TOKENS == 0
    assert all(S % (d * QUERY_BLOCK) == 0 for d in DILATIONS)
    x2 = x.reshape(B * S, D)
    wd_words = _pack_rows(peer_w_down, transpose=False)
    wu_words = _pack_rows(peer_w_up, transpose=True)
    for layer in range(depth):
        j = layer // 2
        if layer % 2 == 0:
            qkv, p_in = _inproj(x2, norm_mix[layer], ab_w_in[j], S, tm)
            outs, lses = zip(*[_attention(qkv, B, S, dil) for dil in DILATIONS])
            x2 = _mix0(outs, lses, p_in, x2, ab_w_out[j], pool_w[j], pool_scale[j], S, tm)
        else:
            x2 = _sgu(x2, norm_mix[layer], c_w_in[j], c_norm[j], c_w_s[j], c_b_s[j], c_w_out[j], tm)
        ht, r2, e2, ne = _peer_gates(x2, norm_ffn[layer], peer_w_q[layer], peer_sub_keys[layer])
        x2 = _peer_experts(ht, r2, e2, ne, x2, wd_words, wu_words, layer, norm_final,
                           layer == depth - 1, tm, eb)
    return x2.reshape(B, S, D)
```

```python
import functools
import math

import jax
import jax.numpy as jnp
import numpy as np
from jax import lax
from jax.experimental import pallas as pl
from jax.experimental.pallas import tpu as pltpu

F32 = jnp.float32
BF16 = jnp.bfloat16

LANES = 128
SUBLANES = 8
BF16_ROWS = 2 * SUBLANES
MXU_WIDTH = 256
VREG_TOKENS = SUBLANES * LANES
VMEM_LIMIT = 56 * 1024 * 1024

ATTN_HEADS = 8
HEAD_DIM = 64
ATTN_WIDTH = ATTN_HEADS * HEAD_DIM
DILATIONS = (1, 4, 16)
ATTN_HALF = 64
QUERY_BLOCK = 128
ROPE_THETA = 500000.0
ROPE_DIM = HEAD_DIM // 4
POOL_WINDOWS = (2, 4, 8, 16)
POOL_HALO = 8
SGU_CHUNK = 128
SGU_GROUPS = 8
PEER_HEADS = 8
PEER_N_KEYS = 128
PEER_TOPK = 16
RMS_EPS = 1e-6
NEG_INF = -1e30
GELU_C = math.sqrt(2.0 / math.pi)


def _params(semantics):
    return pltpu.CompilerParams(dimension_semantics=semantics, vmem_limit_bytes=VMEM_LIMIT)


def _rms(x, g):
    return x * lax.rsqrt(jnp.mean(x * x, axis=-1, keepdims=True) + RMS_EPS) * g


U32 = jnp.uint32


def _as_bf16(words):
    return pltpu.bitcast(words, BF16)


def _as_words(x):
    return pltpu.bitcast(x.astype(BF16), U32)


def _pack_kernel(w_ref, o_ref, *, transpose):
    w = w_ref[...]
    o_ref[...] = _as_words(w.T if transpose else w)


def _pack_rows(w, transpose, rows=512):
    N, R, C = w.shape
    out_block, out_map = ((None, C // 2, rows), lambda n, i: (n, 0, i)) if transpose else (
        (None, rows // 2, C), lambda n, i: (n, i, 0))
    out_shape = (N, C // 2, R) if transpose else (N, R // 2, C)
    return pl.pallas_call(
        functools.partial(_pack_kernel, transpose=transpose),
        grid=(N, R // rows),
        in_specs=[pl.BlockSpec((None, rows, C), lambda n, i: (n, i, 0))],
        out_specs=pl.BlockSpec(out_block, out_map),
        out_shape=jax.ShapeDtypeStruct(out_shape, U32),
        compiler_params=_params(("parallel", "parallel")),
    )(w)


def _gelu(x):
    half = 0.5 * x
    return half + half * jnp.tanh(x * (GELU_C + (GELU_C * 0.044715) * (x * x)))


def _inproj_kernel(x_ref, g_ref, w_ref, cos_ref, sa_ref, sb_ref, qkv_ref, p_ref):
    h = _rms(x_ref[...], g_ref[...]).astype(BF16)
    proj = jnp.dot(h, w_ref[...], preferred_element_type=F32)
    c, sa, sb = cos_ref[...], sa_ref[...], sb_ref[...]
    half = ROPE_DIM // 2
    for col in range(0, 2 * ATTN_WIDTH, LANES):
        t = proj[:, col:col + LANES]
        up = pltpu.roll(t, LANES - half, axis=1)
        dn = pltpu.roll(t, half, axis=1)
        qkv_ref[:, col:col + LANES] = (t * c + up * sa + dn * sb).astype(BF16)
    qkv_ref[:, 2 * ATTN_WIDTH:] = proj[:, 2 * ATTN_WIDTH:3 * ATTN_WIDTH].astype(BF16)
    p_ref[...] = proj[:, 3 * ATTN_WIDTH:]


def _rope_tables(seq):
    half = ROPE_DIM // 2
    pos = jnp.arange(seq, dtype=F32)
    inv_freq = ROPE_THETA ** (-jnp.arange(0, ROPE_DIM, 2, dtype=F32) / ROPE_DIM)
    ang = pos[:, None] * inv_freq[None, :]
    cos, sin = jnp.cos(ang), jnp.sin(ang)
    pad = HEAD_DIM - ROPE_DIM
    ones = jnp.ones((seq, pad), F32)
    zeros = jnp.zeros((seq, pad), F32)
    zh = jnp.zeros((seq, half), F32)
    c = jnp.concatenate([cos, cos, ones], axis=1)
    sa = jnp.concatenate([-sin, zh, zeros], axis=1)
    sb = jnp.concatenate([zh, sin, zeros], axis=1)
    return tuple(jnp.concatenate([t, t], axis=1) for t in (c, sa, sb))


def _inproj(x2, g, w_in, seq, tm):
    T, D = x2.shape
    n_out = w_in.shape[1]
    pool_w = n_out - 3 * ATTN_WIDTH
    c, sa, sb = _rope_tables(seq)
    spt = seq // tm
    tab = pl.BlockSpec((tm, LANES), lambda i: (i % spt, 0))
    return pl.pallas_call(
        _inproj_kernel,
        grid=(T // tm,),
        in_specs=[pl.BlockSpec((tm, D), lambda i: (i, 0)),
                  pl.BlockSpec((1, D), lambda i: (0, 0)),
                  pl.BlockSpec((D, n_out), lambda i: (0, 0)),
                  tab, tab, tab],
        out_specs=[pl.BlockSpec((tm, 3 * ATTN_WIDTH), lambda i: (i, 0)),
                   pl.BlockSpec((tm, pool_w), lambda i: (i, 0))],
        out_shape=[jax.ShapeDtypeStruct((T, 3 * ATTN_WIDTH), BF16),
                   jax.ShapeDtypeStruct((T, pool_w), F32)],
        compiler_params=_params(("parallel",)),
    )(x2, g.reshape(1, D), w_in.astype(BF16), c, sa, sb)


def _attn_kernel(q_ref, k_ref, v_ref, o_ref, lse_ref, *, length):
    kw = min(QUERY_BLOCK + 2 * ATTN_HALF, length)
    scale = HEAD_DIM ** -0.5
    lane = lax.broadcasted_iota(jnp.int32, (QUERY_BLOCK, LANES), 1)
    first_head = lane < HEAD_DIM

    def block(qb, carry):
        q0 = pl.multiple_of(qb * QUERY_BLOCK, QUERY_BLOCK)
        start = pl.multiple_of(jnp.clip(q0 - ATTN_HALF, 0, length - kw), ATTN_HALF)
        qpos = q0 + lax.broadcasted_iota(jnp.int32, (QUERY_BLOCK, kw), 0)
        kpos = start + lax.broadcasted_iota(jnp.int32, (QUERY_BLOCK, kw), 1)
        valid = jnp.abs(kpos - qpos) <= ATTN_HALF
        for col in range(0, ATTN_WIDTH, LANES):
            qp = q_ref[pl.ds(q0, QUERY_BLOCK), col:col + LANES]
            kp = k_ref[pl.ds(start, kw), col:col + LANES]
            vp = v_ref[pl.ds(start, kw), col:col + LANES]
            outs, lses = [], []
            for mask in (first_head, jnp.logical_not(first_head)):
                qm = jnp.where(mask, qp, jnp.zeros_like(qp))
                s = lax.dot_general(qm, kp, (((1,), (1,)), ((), ())),
                                    preferred_element_type=F32) * scale
                s = jnp.where(valid, s, NEG_INF)
                m = jnp.max(s, axis=-1, keepdims=True)
                p = jnp.exp(s - m)
                l = jnp.sum(p, axis=-1, keepdims=True)
                outs.append(jnp.dot(p.astype(BF16), vp, preferred_element_type=F32) / l)
                lses.append(jnp.broadcast_to(m + jnp.log(l), (QUERY_BLOCK, LANES)))
            o_ref[pl.ds(q0, QUERY_BLOCK), col:col + LANES] = (
                jnp.where(first_head, outs[0], outs[1]).astype(BF16))
            lse_ref[pl.ds(q0, QUERY_BLOCK), col:col + LANES] = (
                jnp.where(first_head, lses[0], lses[1]))
        return carry

    lax.fori_loop(0, length // QUERY_BLOCK, block, 0)


def _attention(qkv, batch, seq, dil):
    length = seq // dil
    width = 3 * ATTN_WIDTH
    view = qkv.reshape(batch, length, dil * width)

    def part(j):
        return pl.BlockSpec((None, length, ATTN_WIDTH), lambda b, r: (b, 0, 3 * r + j))

    out = pl.BlockSpec((None, length, ATTN_WIDTH), lambda b, r: (b, 0, r))
    o, lse = pl.pallas_call(
        functools.partial(_attn_kernel, length=length),
        grid=(batch, dil),
        in_specs=[part(0), part(1), part(2)],
        out_specs=[out, out],
        out_shape=[jax.ShapeDtypeStruct((batch, length, dil * ATTN_WIDTH), BF16),
                   jax.ShapeDtypeStruct((batch, length, dil * ATTN_WIDTH), F32)],
        compiler_params=_params(("parallel", "parallel")),
    )(view, view, view)
    return o.reshape(batch * seq, ATTN_WIDTH), lse.reshape(batch * seq, ATTN_WIDTH)


def _mix0_kernel(o1, o2, o3, l1, l2, l3, pc_ref, pp_ref, pn_ref, x_ref, wout_ref, pw_ref, ps_ref,
                 out_ref, ext_ref, *, seq, tm):
    ti = pl.program_id(0) % (seq // tm)
    la, lb, lc = l1[...], l2[...], l3[...]
    m = jnp.maximum(jnp.maximum(la, lb), lc)
    ea, eb, ec = jnp.exp(la - m), jnp.exp(lb - m), jnp.exp(lc - m)
    a = (o1[...].astype(F32) * ea + o2[...].astype(F32) * eb + o3[...].astype(F32) * ec) / (ea + eb + ec)
    acc = jnp.dot(a.astype(BF16), wout_ref[:ATTN_WIDTH, :], preferred_element_type=F32)

    ext_ref[:POOL_HALO, :] = jnp.where(ti > 0, pp_ref[...], 0.0)
    ext_ref[POOL_HALO:POOL_HALO + tm, :] = pc_ref[...]
    ext_ref[POOL_HALO + tm:, :] = jnp.where(ti < seq // tm - 1, pn_ref[...], 0.0)
    pos = ti * tm + lax.broadcasted_iota(jnp.int32, (tm, 1), 0)
    for g, win in enumerate(POOL_WINDOWS):
        cols = slice(g * LANES, (g + 1) * LANES)
        before, after = win // 2, win - win // 2
        tot = ext_ref[POOL_HALO - before:POOL_HALO - before + tm, cols]
        for d in range(-before + 1, after):
            tot = tot + ext_ref[POOL_HALO + d:POOL_HALO + d + tm, cols]
        cnt = (jnp.minimum(pos + after, seq) - jnp.maximum(pos - before, 0)).astype(F32)
        diff = tot / cnt - pc_ref[:, cols]
        mixed = jnp.dot(diff.astype(BF16), pw_ref[g], preferred_element_type=F32) * ps_ref[:, cols]
        acc = acc + jnp.dot(mixed.astype(BF16),
                            wout_ref[ATTN_WIDTH + g * LANES:ATTN_WIDTH + (g + 1) * LANES, :],
                            preferred_element_type=F32)
    out_ref[...] = x_ref[...] + acc


def _mix0(outs, lses, p_in, x2, w_out, pool_w, pool_scale, seq, tm):
    T, D = x2.shape
    pw = p_in.shape[1]
    hb = tm // POOL_HALO
    last = T // POOL_HALO - 1
    row = lambda w: pl.BlockSpec((tm, w), lambda i: (i, 0))
    full = lambda s: pl.BlockSpec(s, lambda i: (0,) * len(s))
    return pl.pallas_call(
        functools.partial(_mix0_kernel, seq=seq, tm=tm),
        grid=(T // tm,),
        in_specs=[row(ATTN_WIDTH)] * 6 + [
            row(pw),
            pl.BlockSpec((POOL_HALO, pw), lambda i: (jnp.maximum(i * hb - 1, 0), 0)),
            pl.BlockSpec((POOL_HALO, pw), lambda i: (jnp.minimum((i + 1) * hb, last), 0)),
            row(D), full(w_out.shape), full(pool_w.shape), full((1, pw))],
        out_specs=row(D),
        out_shape=jax.ShapeDtypeStruct((T, D), F32),
        scratch_shapes=[pltpu.VMEM((tm + 2 * POOL_HALO, pw), F32)],
        compiler_params=_params(("parallel",)),
    )(*outs, *lses, p_in, p_in, p_in, x2, w_out.astype(BF16), pool_w.astype(BF16),
      pool_scale.reshape(1, pw))


def _sgu_kernel(x_ref, g_ref, win_ref, cn_ref, ws_ref, bs_ref, wout_ref, out_ref, gated_ref, *, tm):
    x = x_ref[...]
    width = wout_ref.shape[0]
    h = _rms(x, g_ref[...]).astype(BF16)
    u = _gelu(jnp.dot(h, win_ref[:, :width], preferred_element_type=F32))
    v = _gelu(jnp.dot(h, win_ref[:, width:], preferred_element_type=F32))
    vb = _rms(v, cn_ref[...]).astype(BF16)
    for n in range(tm // SGU_CHUNK):
        rows = slice(n * SGU_CHUNK, (n + 1) * SGU_CHUNK)
        for g in range(SGU_GROUPS):
            cols = slice(g * LANES, (g + 1) * LANES)
            mixed = jnp.dot(ws_ref[g], vb[rows, cols], preferred_element_type=F32) + bs_ref[:, cols]
            gated_ref[rows, cols] = (u[rows, cols] * mixed).astype(BF16)
    out_ref[...] = x + jnp.dot(gated_ref[...], wout_ref[...], preferred_element_type=F32)


def _sgu(x2, g, w_in, c_norm, w_s, b_s, w_out, tm):
    T, D = x2.shape
    width = w_out.shape[0]
    bias = jnp.repeat(b_s.T, width // SGU_GROUPS, axis=1)
    full = lambda s: pl.BlockSpec(s, lambda i: (0,) * len(s))
    return pl.pallas_call(
        functools.partial(_sgu_kernel, tm=tm),
        grid=(T // tm,),
        in_specs=[pl.BlockSpec((tm, D), lambda i: (i, 0)), full((1, D)), full(w_in.shape),
                  full((1, width)), full(w_s.shape), full(bias.shape), full(w_out.shape)],
        out_specs=pl.BlockSpec((tm, D), lambda i: (i, 0)),
        out_shape=jax.ShapeDtypeStruct((T, D), F32),
        scratch_shapes=[pltpu.VMEM((tm, width), BF16)],
        compiler_params=_params(("parallel",)),
    )(x2, g.reshape(1, D), w_in.astype(BF16), c_norm.reshape(1, width), w_s.astype(BF16), bias,
      w_out.astype(BF16))


def _cmpx(v, i, l, descending):
    hi, lo = jnp.maximum(v[i], v[l]), jnp.minimum(v[i], v[l])
    v[i], v[l] = (hi, lo) if descending else (lo, hi)


def _bitonic_sort_desc(v):
    v = list(v)
    n = len(v)
    k = 2
    while k <= n:
        j = k // 2
        while j >= 1:
            for i in range(n):
                l = i ^ j
                if l > i:
                    _cmpx(v, i, l, (i & k) == 0)
            j //= 2
        k *= 2
    return v


def _merge_top(a, b):
    n = len(a)
    v = [jnp.maximum(a[i], b[n - 1 - i]) for i in range(n)]
    j = n // 2
    while j >= 1:
        for i in range(n):
            if (i & j) == 0:
                _cmpx(v, i, i + j, True)
        j //= 2
    return v


def _top_sorted(vals, k):
    groups = [_bitonic_sort_desc(vals[i:i + k]) for i in range(0, len(vals), k)]
    while len(groups) > 1:
        groups = [_merge_top(groups[i], groups[i + 1]) for i in range(0, len(groups), 2)]
    return groups[0]


def _count_reached(s, t):
    assert len(t) == 16
    c16 = s >= t[15]
    c8 = s >= t[7]
    c4 = s >= jnp.where(c8, t[11], t[3])
    c2 = s >= jnp.where(c8, jnp.where(c4, t[13], t[9]), jnp.where(c4, t[5], t[1]))
    quarter = [jnp.where(c2, t[4 * q + 2], t[4 * q]) for q in range(4)]
    c1 = s >= jnp.where(c8, jnp.where(c4, quarter[3], quarter[2]), jnp.where(c4, quarter[1], quarter[0]))
    return (jnp.where(c8, 8.0, 0.0) + jnp.where(c4, 4.0, 0.0) + jnp.where(c2, 2.0, 0.0)
            + jnp.where(c1, 1.0, 0.0) + jnp.where(c16, 1.0, 0.0))


def _count(preds):
    tot = jnp.where(preds[0], 1.0, 0.0)
    for p in preds[1:]:
        tot = tot + jnp.where(p, 1.0, 0.0)
    return tot


def _twin_bf16(v):
    bits = pltpu.bitcast(v.astype(BF16).astype(F32), U32)
    return bits | (bits >> 16)


def _peer_gate_kernel(x_ref, g_ref, wq_ref, keys_ref, ht_ref, r2_ref, e2_ref, ne_ref,
                      km_ref, om_ref, *, tg):
    K = PEER_TOPK
    nk = PEER_N_KEYS
    chunks = tg // LANES

    @pl.when(pl.program_id(1) == 0)
    def _():
        ht_ref[...] = _as_words(_rms(x_ref[...], g_ref[...]).T)

    qt = jnp.dot(wq_ref[...], _as_bf16(ht_ref[...]), preferred_element_type=F32)
    half_dim = qt.shape[0] // 2
    for side in range(2):
        sc = jnp.dot(keys_ref[side], qt[side * half_dim:(side + 1) * half_dim].astype(BF16),
                     preferred_element_type=F32)
        for c in range(chunks):
            km_ref[side, pl.ds(c, nk, stride=chunks), :] = sc[:, c * LANES:(c + 1) * LANES]

    def keyrows(k):
        return slice(k * chunks, (k + 1) * chunks)

    s1 = [km_ref[0, keyrows(k), :] for k in range(nk)]
    s2 = [km_ref[1, keyrows(k), :] for k in range(nk)]
    A = _top_sorted(s1, K)
    B = _top_sorted(s2, K)

    reach = [K // (i + 1) for i in range(K)]
    cand = [[A[i] + B[j] for j in range(reach[i])] for i in range(K)]
    ninf = jnp.full_like(A[0], -jnp.inf)
    rest = [c for row in cand[1:] for c in row]
    rest = rest + [ninf] * (-len(rest) % K)
    top = cand[0]
    for i in range(0, len(rest), K):
        top = _merge_top(top, _bitonic_sort_desc(rest[i:i + K]))
    thr = top[K - 1]

    gt = [_count([c > thr for c in row]) for row in cand]
    eq = [_count([c == thr for c in row]) for row in cand]
    total_gt = functools.reduce(lambda a, b: a + b, gt)
    room = float(K) - total_gt
    n = []
    for i in range(K):
        n.append(gt[i] + jnp.clip(room, 0.0, eq[i]))
        room = room - eq[i]

    e1 = [jnp.exp(a - A[0]) for a in A]
    e2 = [jnp.exp(b - B[0]) for b in B]
    z = jnp.zeros_like(A[0])
    for i in range(K):
        inner = jnp.zeros_like(z)
        for j in range(reach[i]):
            inner = inner + jnp.where(n[i] > float(j), e2[j], 0.0)
        z = z + e1[i] * inner
    inv_z = 1.0 / z

    pinf = jnp.full_like(A[0], jnp.inf)
    t = []
    for m in range(1, K + 1):
        tm_ = pinf
        for i in range(K):
            tm_ = jnp.minimum(tm_, jnp.where(n[i] >= float(m), A[i], pinf))
        t.append(tm_)

    b_up = B[::-1]
    for k in range(nk):
        om_ref[0, keyrows(k), :] = _count_reached(s1[k], t)
        om_ref[1, keyrows(k), :] = jnp.exp(s1[k] - A[0]) * inv_z
        om_ref[2, keyrows(k), :] = float(K) - _count_reached(s2[k], b_up)
        om_ref[3, keyrows(k), :] = jnp.exp(s2[k] - B[0])

    for c in range(chunks):
        cols = slice(c * LANES, (c + 1) * LANES)
        ne_ref[0, c] = _twin_bf16(om_ref[0, pl.ds(c, nk, stride=chunks), :])
        ne_ref[1, c] = _twin_bf16(om_ref[1, pl.ds(c, nk, stride=chunks), :])
        r2_ref[:, cols] = _as_words(om_ref[2, pl.ds(c, nk, stride=chunks), :])
        e2_ref[:, cols] = _as_words(om_ref[3, pl.ds(c, nk, stride=chunks), :])


def _peer_gates(x2, g, w_q, sub_keys):
    T, D = x2.shape
    tg = VREG_TOKENS
    heads, _, nk, half_dim = sub_keys.shape
    wq_t = w_q.T.astype(BF16)
    return pl.pallas_call(
        functools.partial(_peer_gate_kernel, tg=tg),
        grid=(T // tg, heads),
        in_specs=[pl.BlockSpec((tg, D), lambda i, h: (i, 0)),
                  pl.BlockSpec((1, D), lambda i, h: (0, 0)),
                  pl.BlockSpec((2 * half_dim, D), lambda i, h: (h, 0)),
                  pl.BlockSpec((None, 2, nk, half_dim), lambda i, h: (h, 0, 0, 0))],
        out_specs=[pl.BlockSpec((D // 2, tg), lambda i, h: (0, i)),
                   pl.BlockSpec((nk // 2, tg), lambda i, h: (h, i)),
                   pl.BlockSpec((nk // 2, tg), lambda i, h: (h, i)),
                   pl.BlockSpec((None, 2, tg // LANES, nk, LANES), lambda i, h: (h, 0, i, 0, 0))],
        out_shape=[jax.ShapeDtypeStruct((D // 2, T), U32),
                   jax.ShapeDtypeStruct((heads * nk // 2, T), U32),
                   jax.ShapeDtypeStruct((heads * nk // 2, T), U32),
                   jax.ShapeDtypeStruct((heads, 2, T // LANES, nk, LANES), U32)],
        scratch_shapes=[pltpu.VMEM((2, nk * SUBLANES, LANES), F32),
                        pltpu.VMEM((4, nk * SUBLANES, LANES), F32)],
        compiler_params=_params(("parallel", "arbitrary")),
    )(x2, g.reshape(1, D), wq_t, sub_keys.astype(BF16))


KEY_BLOCK = 2
PIECE_ROWS = 256


def _gate_hidden(hid_ref, ga_ref, r2_ref, e2_ref, ne_ref, first_key, c, al0):
    nk = PEER_N_KEYS
    tiles = range(0, nk, BF16_ROWS)
    cols = slice(c * LANES, (c + 1) * LANES)

    def row_tile(h, q, al):
        return _as_bf16(ne_ref[h, q, c, pl.ds(first_key + al, SUBLANES, stride=0), :])

    gates = [[None for _ in tiles] for _ in range(KEY_BLOCK)]
    for h in range(PEER_HEADS):
        counts = [row_tile(h, 0, al0 + k) for k in range(KEY_BLOCK)]
        factors = [row_tile(h, 1, al0 + k) for k in range(KEY_BLOCK)]
        for ti, r0 in enumerate(tiles):
            hrows = slice((h * nk + r0) // 2, (h * nk + r0) // 2 + SUBLANES)
            rank, e2 = _as_bf16(r2_ref[hrows, cols]), _as_bf16(e2_ref[hrows, cols])
            for k in range(KEY_BLOCK):
                term = jnp.where(rank < counts[k], e2, jnp.zeros_like(e2)) * factors[k]
                gates[k][ti] = term if gates[k][ti] is None else gates[k][ti] + term
    for k in range(KEY_BLOCK):
        for ti, r0 in enumerate(tiles):
            rows = slice((al0 + k) * nk + r0, (al0 + k) * nk + r0 + BF16_ROWS)
            ga_ref[rows, cols] = gates[k][ti] * _gelu(hid_ref[rows, cols].astype(BF16))


def _peer_expert_kernel(ht_ref, wd_ref, wu_ref, r2_ref, e2_ref, ne_ref, x_ref, gf_ref, out_ref,
                        acc_ref, hid0_ref, hid1_ref, ga0_ref, ga1_ref, *, final_norm, n_blocks, n_work):
    s = pl.program_id(0)
    eb = hid0_ref.shape[0]
    gate_item = jnp.clip(s - 1, 0, n_work - 1)
    up_block = jnp.clip(s - 2, 0, n_work - 1) % n_blocks
    first_key = (gate_item % n_blocks) * (eb // PEER_N_KEYS)

    @pl.when(s == 0)
    def _():
        hid1_ref[...] = jnp.zeros_like(hid1_ref)
        ga0_ref[...] = jnp.zeros_like(ga0_ref)

    @pl.when(up_block == 0)
    def _():
        acc_ref[...] = jnp.zeros_like(acc_ref)

    def stages(hid_new, hid_old, ga_new, ga_old):
        tm = hid_new.shape[1]
        d_model = acc_ref.shape[0]

        def up(n, m):
            cols = slice(n * MXU_WIDTH, (n + 1) * MXU_WIDTH)
            rows = slice(m * PIECE_ROWS, (m + 1) * PIECE_ROWS)
            words = slice(m * PIECE_ROWS // 2, (m + 1) * PIECE_ROWS // 2)
            acc_ref[rows, cols] += jnp.dot(_as_bf16(wu_ref[words, :]), ga_old[:, cols],
                                           preferred_element_type=F32)

        def down(n, m):
            cols = slice(n * MXU_WIDTH, (n + 1) * MXU_WIDTH)
            rows = slice(m * PIECE_ROWS, (m + 1) * PIECE_ROWS)
            words = slice(m * PIECE_ROWS // 2, (m + 1) * PIECE_ROWS // 2)
            hid_new[rows, cols] = jnp.dot(_as_bf16(wd_ref[words, :]), _as_bf16(ht_ref[:, cols]),
                                          preferred_element_type=F32)

        mxu = [functools.partial(f, n, m) for n in range(tm // MXU_WIDTH)
               for f, size in ((up, d_model), (down, eb)) for m in range(size // PIECE_ROWS)]
        vpu = [functools.partial(_gate_hidden, hid_old, ga_new, r2_ref, e2_ref, ne_ref, first_key, c, al0)
               for c in range(tm // LANES) for al0 in range(0, eb // PEER_N_KEYS, KEY_BLOCK)]
        done = 0
        for i, piece in enumerate(mxu):
            piece()
            upto = (i + 1) * len(vpu) // len(mxu)
            for g in vpu[done:upto]:
                g()
            done = upto

    pl.when(s % 2 == 0)(lambda: stages(hid0_ref, hid1_ref, ga1_ref, ga0_ref))
    pl.when(s % 2 == 1)(lambda: stages(hid1_ref, hid0_ref, ga0_ref, ga1_ref))

    @pl.when((s >= 2) & (up_block == n_blocks - 1))
    def _():
        y = x_ref[...] + acc_ref[...].T
        out_ref[...] = _rms(y, gf_ref[...]) if final_norm else y


def _peer_experts(ht, r2, e2, ne, x2, wd_words, wu_words, layer, g_final, final_norm, tm, eb):
    T, D = x2.shape
    n_exp = wu_words.shape[2]
    heads, _, _, nk, _ = ne.shape
    n_blocks = n_exp // eb
    n_work = (T // tm) * n_blocks

    def item(lag):
        def split(s):
            w = jnp.clip(s - lag, 0, n_work - 1)
            return w // n_blocks, w % n_blocks
        return split

    down, gate, up = item(0), item(1), item(2)
    return pl.pallas_call(
        functools.partial(_peer_expert_kernel, final_norm=final_norm, n_blocks=n_blocks, n_work=n_work),
        grid=(n_work + 2,),
        in_specs=[pl.BlockSpec((D // 2, tm), lambda s: (0, down(s)[0])),
                  pl.BlockSpec((None, eb // 2, D), lambda s: (layer, down(s)[1], 0)),
                  pl.BlockSpec((None, D // 2, eb), lambda s: (layer, 0, up(s)[1])),
                  pl.BlockSpec((heads * nk // 2, tm), lambda s: (0, gate(s)[0])),
                  pl.BlockSpec((heads * nk // 2, tm), lambda s: (0, gate(s)[0])),
                  pl.BlockSpec((heads, 2, tm // LANES, nk, LANES), lambda s: (0, 0, gate(s)[0], 0, 0)),
                  pl.BlockSpec((tm, D), lambda s: (up(s)[0], 0)),
                  pl.BlockSpec((1, D), lambda s: (0, 0))],
        out_specs=pl.BlockSpec((tm, D), lambda s: (up(s)[0], 0)),
        out_shape=jax.ShapeDtypeStruct((T, D), F32),
        scratch_shapes=[pltpu.VMEM((D, tm), F32), pltpu.VMEM((eb, tm), F32), pltpu.VMEM((eb, tm), F32),
                        pltpu.VMEM((eb, tm), BF16), pltpu.VMEM((eb, tm), BF16)],
        compiler_params=_params(("arbitrary",)),
    )(ht, wd_words, wu_words, r2, e2, ne, x2, g_final.reshape(1, D))


def kernel(x, norm_mix, norm_ffn, norm_final, ab_w_in, ab_w_out, pool_w, pool_scale, c_w_in,
           c_norm, c_w_s, c_b_s, c_w_out, peer_w_q, peer_sub_keys, peer_w_down, peer_w_up):
    B, S, D = x.shape
    depth = norm_mix.shape[0]
    tm = 512
    eb = 2048
    assert S % tm == 0 and (B * S) % VREG_TOKENS == 0
    assert all(S % (d * QUERY_BLOCK) == 0 for d in DILATIONS)
    x2 = x.reshape(B * S, D)
    wd_words = _pack_rows(peer_w_down, transpose=False)
    wu_words = _pack_rows(peer_w_up, transpose=True)
    for layer in range(depth):
        j = layer // 2
        if layer % 2 == 0:
            qkv, p_in = _inproj(x2, norm_mix[layer], ab_w_in[j], S, tm)
            outs, lses = zip(*[_attention(qkv, B, S, dil) for dil in DILATIONS])
            x2 = _mix0(outs, lses, p_in, x2, ab_w_out[j], pool_w[j], pool_scale[j], S, tm)
        else:
            x2 = _sgu(x2, norm_mix[layer], c_w_in[j], c_norm[j], c_w_s[j], c_b_s[j], c_w_out[j], tm)
        ht, r2, e2, ne = _peer_gates(x2, norm_ffn[layer], peer_w_q[layer], peer_sub_keys[layer])
        x2 = _peer_experts(ht, r2, e2, ne, x2, wd_words, wu_words, layer, norm_final,
                           layer == depth - 1, tm, eb)
    return x2.reshape(B, S, D)
```

```python
import functools
import math

import jax
import jax.numpy as jnp
import numpy as np
from jax import lax
from jax.experimental import pallas as pl
from jax.experimental.pallas import tpu as pltpu

F32 = jnp.float32
BF16 = jnp.bfloat16

LANES = 128
SUBLANES = 8
BF16_ROWS = 2 * SUBLANES
MXU_WIDTH = 256
VREG_TOKENS = SUBLANES * LANES
VMEM_LIMIT = 56 * 1024 * 1024

ATTN_HEADS = 8
HEAD_DIM = 64
ATTN_WIDTH = ATTN_HEADS * HEAD_DIM
DILATIONS = (1, 4, 16)
ATTN_HALF = 64
QUERY_BLOCK = 128
BLOCK_UNROLL = 4
ROPE_THETA = 500000.0
ROPE_DIM = HEAD_DIM // 4
POOL_WINDOWS = (2, 4, 8, 16)
POOL_HALO = 8
SGU_CHUNK = 128
SGU_GROUPS = 8
PEER_HEADS = 8
PEER_N_KEYS = 128
PEER_TOPK = 16
RMS_EPS = 1e-6
NEG_INF = -1e30
GELU_C = math.sqrt(2.0 / math.pi)


def _params(semantics):
    return pltpu.CompilerParams(dimension_semantics=semantics, vmem_limit_bytes=VMEM_LIMIT)


def _rms(x, g):
    return x * lax.rsqrt(jnp.mean(x * x, axis=-1, keepdims=True) + RMS_EPS) * g


U32 = jnp.uint32


def _as_bf16(words):
    return pltpu.bitcast(words, BF16)


def _as_words(x):
    return pltpu.bitcast(x.astype(BF16), U32)


def _pack_kernel(w_ref, o_ref, *, transpose):
    w = w_ref[...]
    o_ref[...] = _as_words(w.T if transpose else w)


def _pack_rows(w, transpose, rows=512):
    N, R, C = w.shape
    out_block, out_map = ((None, C // 2, rows), lambda n, i: (n, 0, i)) if transpose else (
        (None, rows // 2, C), lambda n, i: (n, i, 0))
    out_shape = (N, C // 2, R) if transpose else (N, R // 2, C)
    return pl.pallas_call(
        functools.partial(_pack_kernel, transpose=transpose),
        grid=(N, R // rows),
        in_specs=[pl.BlockSpec((None, rows, C), lambda n, i: (n, i, 0))],
        out_specs=pl.BlockSpec(out_block, out_map),
        out_shape=jax.ShapeDtypeStruct(out_shape, U32),
        compiler_params=_params(("parallel", "parallel")),
    )(w)


def _gelu(x):
    half = 0.5 * x
    return half + half * jnp.tanh(x * (GELU_C + (GELU_C * 0.044715) * (x * x)))


def _inproj_kernel(x_ref, g_ref, w_ref, cos_ref, sa_ref, sb_ref, qkv_ref, p_ref):
    h = _rms(x_ref[...], g_ref[...]).astype(BF16)
    proj = jnp.dot(h, w_ref[...], preferred_element_type=F32)
    c, sa, sb = cos_ref[...], sa_ref[...], sb_ref[...]
    half = ROPE_DIM // 2
    for slab in range(2 * ATTN_WIDTH // LANES):
        t = proj[:, slab * LANES:(slab + 1) * LANES]
        up = pltpu.roll(t, LANES - half, axis=1)
        dn = pltpu.roll(t, half, axis=1)
        qkv_ref[slab] = t * c + up * sa + dn * sb
    for slab in range(2 * ATTN_WIDTH // LANES, 3 * ATTN_WIDTH // LANES):
        qkv_ref[slab] = proj[:, slab * LANES:(slab + 1) * LANES]
    p_ref[...] = proj[:, 3 * ATTN_WIDTH:]


def _rope_tables(seq):
    half = ROPE_DIM // 2
    pos = jnp.arange(seq, dtype=F32)
    inv_freq = ROPE_THETA ** (-jnp.arange(0, ROPE_DIM, 2, dtype=F32) / ROPE_DIM)
    ang = pos[:, None] * inv_freq[None, :]
    cos, sin = jnp.cos(ang), jnp.sin(ang)
    pad = HEAD_DIM - ROPE_DIM
    ones = jnp.ones((seq, pad), F32)
    zeros = jnp.zeros((seq, pad), F32)
    zh = jnp.zeros((seq, half), F32)
    c = jnp.concatenate([cos, cos, ones], axis=1)
    sa = jnp.concatenate([-sin, zh, zeros], axis=1)
    sb = jnp.concatenate([zh, sin, zeros], axis=1)
    return tuple(jnp.concatenate([t, t], axis=1) for t in (c, sa, sb))


def _inproj(x2, g, w_in, seq, tm):
    T, D = x2.shape
    n_out = w_in.shape[1]
    pool_w = n_out - 3 * ATTN_WIDTH
    c, sa, sb = _rope_tables(seq)
    spt = seq // tm
    tab = pl.BlockSpec((tm, LANES), lambda i: (i % spt, 0))
    return pl.pallas_call(
        _inproj_kernel,
        grid=(T // tm,),
        in_specs=[pl.BlockSpec((tm, D), lambda i: (i, 0)),
                  pl.BlockSpec((1, D), lambda i: (0, 0)),
                  pl.BlockSpec((D, n_out), lambda i: (0, 0)),
                  tab, tab, tab],
        out_specs=[pl.BlockSpec((3 * ATTN_WIDTH // LANES, tm, LANES), lambda i: (0, i, 0)),
                   pl.BlockSpec((tm, pool_w), lambda i: (i, 0))],
        out_shape=[jax.ShapeDtypeStruct((3 * ATTN_WIDTH // LANES, T, LANES), F32),
                   jax.ShapeDtypeStruct((T, pool_w), F32)],
        compiler_params=_params(("parallel",)),
    )(x2, g.reshape(1, D), w_in.astype(BF16), c, sa, sb)


def _attn_kernel(q_ref, k_ref, v_ref, a_ref, o_scr, lse_scr, *, seq):
    scale = HEAD_DIM ** -0.5
    lane = lax.broadcasted_iota(jnp.int32, (QUERY_BLOCK, LANES), 1)
    first_head = lane < HEAD_DIM
    head_masks = (first_head, jnp.logical_not(first_head))

    for pat, dil in enumerate(DILATIONS):
        length = seq // dil
        kw = min(QUERY_BLOCK + 2 * ATTN_HALF, length)
        blocks = length // QUERY_BLOCK
        ones = jnp.ones((kw, LANES), BF16)

        def group(g, carry, dil=dil, length=length, kw=kw, blocks=blocks, pat=pat, ones=ones):
            loaded = []
            for j in range(BLOCK_UNROLL):
                it = g * BLOCK_UNROLL + j
                res, qb = it // blocks, it % blocks
                q0 = qb * QUERY_BLOCK
                start = jnp.clip(q0 - ATTN_HALF, 0, length - kw)

                def rows(ref, first, count, res=res):
                    return ref[pl.ds(res + dil * first, count, stride=dil), :]

                qpos = q0 + lax.broadcasted_iota(jnp.int32, (QUERY_BLOCK, kw), 0)
                kpos = start + lax.broadcasted_iota(jnp.int32, (QUERY_BLOCK, kw), 1)
                loaded.append((rows(q_ref, q0, QUERY_BLOCK).astype(BF16), rows(k_ref, start, kw).astype(BF16),
                               rows(v_ref, start, kw).astype(BF16), jnp.abs(kpos - qpos) <= ATTN_HALF,
                               pl.ds(res + dil * q0, QUERY_BLOCK, stride=dil)))
            chains = [(j, mask) for j in range(BLOCK_UNROLL) for mask in head_masks]
            scores = []
            for j, mask in chains:
                qp, kp, _, valid, _ = loaded[j]
                qm = jnp.where(mask, qp, jnp.zeros_like(qp))
                s = lax.dot_general(qm, kp, (((1,), (1,)), ((), ())), preferred_element_type=F32) * scale
                scores.append(jnp.where(valid, s, NEG_INF))
            tops = [jnp.max(s, axis=-1, keepdims=True) for s in scores]
            probs = [jnp.exp(s - m).astype(BF16) for s, m in zip(scores, tops)]
            totals = [jnp.dot(p, ones, preferred_element_type=F32) for p in probs]
            mixes = [jnp.dot(p, loaded[j][2], preferred_element_type=F32) for p, (j, _) in zip(probs, chains)]
            outs = [o / t for o, t in zip(mixes, totals)]
            lses = [m + jnp.log(t) for m, t in zip(tops, totals)]
            for j in range(BLOCK_UNROLL):
                dst = loaded[j][4]
                o_scr[pat, dst, :] = jnp.where(first_head, outs[2 * j], outs[2 * j + 1])
                lse_scr[pat, dst, :] = jnp.where(first_head, lses[2 * j], lses[2 * j + 1])
            return carry

        assert (dil * blocks) % BLOCK_UNROLL == 0
        lax.fori_loop(0, dil * blocks // BLOCK_UNROLL, group, 0)

    la, lb, lc = lse_scr[0], lse_scr[1], lse_scr[2]
    m = jnp.maximum(jnp.maximum(la, lb), lc)
    ea, eb, ec = jnp.exp(la - m), jnp.exp(lb - m), jnp.exp(lc - m)
    a_ref[...] = ((o_scr[0] * ea + o_scr[1] * eb + o_scr[2] * ec) / (ea + eb + ec)).astype(BF16)


def _attention(qkv, batch, seq):
    pairs = ATTN_WIDTH // LANES

    def part(j):
        return pl.BlockSpec((None, seq, LANES), lambda b, hp: (j * pairs + hp, b, 0))

    return pl.pallas_call(
        functools.partial(_attn_kernel, seq=seq),
        grid=(batch, pairs),
        in_specs=[part(0), part(1), part(2)],
        out_specs=pl.BlockSpec((seq, LANES), lambda b, hp: (b, hp)),
        out_shape=jax.ShapeDtypeStruct((batch * seq, ATTN_WIDTH), BF16),
        scratch_shapes=[pltpu.VMEM((len(DILATIONS), seq, LANES), F32),
                        pltpu.VMEM((len(DILATIONS), seq, LANES), F32)],
        compiler_params=_params(("parallel", "parallel")),
    )(qkv, qkv, qkv)


def _mix0_kernel(a_ref, pc_ref, pp_ref, pn_ref, x_ref, wout_ref, pw_ref, ps_ref,
                 out_ref, ext_ref, *, seq, tm):
    ti = pl.program_id(0) % (seq // tm)
    acc = jnp.dot(a_ref[...], wout_ref[:ATTN_WIDTH, :], preferred_element_type=F32)

    ext_ref[:POOL_HALO, :] = jnp.where(ti > 0, pp_ref[...], 0.0)
    ext_ref[POOL_HALO:POOL_HALO + tm, :] = pc_ref[...]
    ext_ref[POOL_HALO + tm:, :] = jnp.where(ti < seq // tm - 1, pn_ref[...], 0.0)
    pos = ti * tm + lax.broadcasted_iota(jnp.int32, (tm, 1), 0)
    for g, win in enumerate(POOL_WINDOWS):
        cols = slice(g * LANES, (g + 1) * LANES)
        before, after = win // 2, win - win // 2
        tot = ext_ref[POOL_HALO - before:POOL_HALO - before + tm, cols]
        for d in range(-before + 1, after):
            tot = tot + ext_ref[POOL_HALO + d:POOL_HALO + d + tm, cols]
        cnt = (jnp.minimum(pos + after, seq) - jnp.maximum(pos - before, 0)).astype(F32)
        diff = tot / cnt - pc_ref[:, cols]
        mixed = jnp.dot(diff.astype(BF16), pw_ref[g], preferred_element_type=F32) * ps_ref[:, cols]
        acc = acc + jnp.dot(mixed.astype(BF16),
                            wout_ref[ATTN_WIDTH + g * LANES:ATTN_WIDTH + (g + 1) * LANES, :],
                            preferred_element_type=F32)
    out_ref[...] = x_ref[...] + acc


def _mix0(a_out, p_in, x2, w_out, pool_w, pool_scale, seq, tm):
    T, D = x2.shape
    pw = p_in.shape[1]
    hb = tm // POOL_HALO
    last = T // POOL_HALO - 1
    row = lambda w: pl.BlockSpec((tm, w), lambda i: (i, 0))
    full = lambda s: pl.BlockSpec(s, lambda i: (0,) * len(s))
    return pl.pallas_call(
        functools.partial(_mix0_kernel, seq=seq, tm=tm),
        grid=(T // tm,),
        in_specs=[
            row(ATTN_WIDTH), row(pw),
            pl.BlockSpec((POOL_HALO, pw), lambda i: (jnp.maximum(i * hb - 1, 0), 0)),
            pl.BlockSpec((POOL_HALO, pw), lambda i: (jnp.minimum((i + 1) * hb, last), 0)),
            row(D), full(w_out.shape), full(pool_w.shape), full((1, pw))],
        out_specs=row(D),
        out_shape=jax.ShapeDtypeStruct((T, D), F32),
        scratch_shapes=[pltpu.VMEM((tm + 2 * POOL_HALO, pw), F32)],
        compiler_params=_params(("parallel",)),
    )(a_out, p_in, p_in, p_in, x2, w_out.astype(BF16), pool_w.astype(BF16),
      pool_scale.reshape(1, pw))


def _sgu_kernel(x_ref, g_ref, win_ref, cn_ref, ws_ref, bs_ref, wout_ref, out_ref, gated_ref, *, tm):
    x = x_ref[...]
    width = wout_ref.shape[0]
    h = _rms(x, g_ref[...]).astype(BF16)
    u = _gelu(jnp.dot(h, win_ref[:, :width], preferred_element_type=F32))
    v = _gelu(jnp.dot(h, win_ref[:, width:], preferred_element_type=F32))
    vb = _rms(v, cn_ref[...]).astype(BF16)
    for n in range(tm // SGU_CHUNK):
        rows = slice(n * SGU_CHUNK, (n + 1) * SGU_CHUNK)
        for g in range(SGU_GROUPS):
            cols = slice(g * LANES, (g + 1) * LANES)
            mixed = jnp.dot(ws_ref[g], vb[rows, cols], preferred_element_type=F32) + bs_ref[:, cols]
            gated_ref[rows, cols] = (u[rows, cols] * mixed).astype(BF16)
    out_ref[...] = x + jnp.dot(gated_ref[...], wout_ref[...], preferred_element_type=F32)


def _sgu(x2, g, w_in, c_norm, w_s, b_s, w_out, tm):
    T, D = x2.shape
    width = w_out.shape[0]
    bias = jnp.repeat(b_s.T, width // SGU_GROUPS, axis=1)
    full = lambda s: pl.BlockSpec(s, lambda i: (0,) * len(s))
    return pl.pallas_call(
        functools.partial(_sgu_kernel, tm=tm),
        grid=(T // tm,),
        in_specs=[pl.BlockSpec((tm, D), lambda i: (i, 0)), full((1, D)), full(w_in.shape),
                  full((1, width)), full(w_s.shape), full(bias.shape), full(w_out.shape)],
        out_specs=pl.BlockSpec((tm, D), lambda i: (i, 0)),
        out_shape=jax.ShapeDtypeStruct((T, D), F32),
        scratch_shapes=[pltpu.VMEM((tm, width), BF16)],
        compiler_params=_params(("parallel",)),
    )(x2, g.reshape(1, D), w_in.astype(BF16), c_norm.reshape(1, width), w_s.astype(BF16), bias,
      w_out.astype(BF16))


def _cmpx(v, i, l, descending):
    hi, lo = jnp.maximum(v[i], v[l]), jnp.minimum(v[i], v[l])
    v[i], v[l] = (hi, lo) if descending else (lo, hi)


def _bitonic_sort_desc(v):
    v = list(v)
    n = len(v)
    k = 2
    while k <= n:
        j = k // 2
        while j >= 1:
            for i in range(n):
                l = i ^ j
                if l > i:
                    _cmpx(v, i, l, (i & k) == 0)
            j //= 2
        k *= 2
    return v


def _merge_top(a, b):
    n = len(a)
    v = [jnp.maximum(a[i], b[n - 1 - i]) for i in range(n)]
    j = n // 2
    while j >= 1:
        for i in range(n):
            if (i & j) == 0:
                _cmpx(v, i, i + j, True)
        j //= 2
    return v


def _top_sorted(vals, k):
    groups = [_bitonic_sort_desc(vals[i:i + k]) for i in range(0, len(vals), k)]
    while len(groups) > 1:
        groups = [_merge_top(groups[i], groups[i + 1]) for i in range(0, len(groups), 2)]
    return groups[0]


def _count_reached(s, t):
    assert len(t) == 16
    c16 = s >= t[15]
    c8 = s >= t[7]
    c4 = s >= jnp.where(c8, t[11], t[3])
    c2 = s >= jnp.where(c8, jnp.where(c4, t[13], t[9]), jnp.where(c4, t[5], t[1]))
    quarter = [jnp.where(c2, t[4 * q + 2], t[4 * q]) for q in range(4)]
    c1 = s >= jnp.where(c8, jnp.where(c4, quarter[3], quarter[2]), jnp.where(c4, quarter[1], quarter[0]))
    return (jnp.where(c8, 8.0, 0.0) + jnp.where(c4, 4.0, 0.0) + jnp.where(c2, 2.0, 0.0)
            + jnp.where(c1, 1.0, 0.0) + jnp.where(c16, 1.0, 0.0))


def _count(preds):
    tot = jnp.where(preds[0], 1.0, 0.0)
    for p in preds[1:]:
        tot = tot + jnp.where(p, 1.0, 0.0)
    return tot


def _twin_bf16(v):
    bits = pltpu.bitcast(v.astype(BF16).astype(F32), U32)
    return bits | (bits >> 16)


def _peer_gate_kernel(x_ref, g_ref, wq_ref, keys_ref, ht_ref, r2_ref, e2_ref, ne_ref,
                      km_ref, om_ref, *, tg):
    K = PEER_TOPK
    nk = PEER_N_KEYS
    chunks = tg // LANES

    @pl.when(pl.program_id(1) == 0)
    def _():
        ht_ref[...] = _as_words(_rms(x_ref[...], g_ref[...]).T)

    qt = jnp.dot(wq_ref[...], _as_bf16(ht_ref[...]), preferred_element_type=F32)
    half_dim = qt.shape[0] // 2
    for side in range(2):
        sc = jnp.dot(keys_ref[side], qt[side * half_dim:(side + 1) * half_dim].astype(BF16),
                     preferred_element_type=F32)
        for c in range(chunks):
            km_ref[side, pl.ds(c, nk, stride=chunks), :] = sc[:, c * LANES:(c + 1) * LANES]

    def keyrows(k):
        return slice(k * chunks, (k + 1) * chunks)

    s1 = [km_ref[0, keyrows(k), :] for k in range(nk)]
    s2 = [km_ref[1, keyrows(k), :] for k in range(nk)]
    A = _top_sorted(s1, K)
    B = _top_sorted(s2, K)

    reach = [K // (i + 1) for i in range(K)]
    cand = [[A[i] + B[j] for j in range(reach[i])] for i in range(K)]
    ninf = jnp.full_like(A[0], -jnp.inf)
    rest = [c for row in cand[1:] for c in row]
    rest = rest + [ninf] * (-len(rest) % K)
    top = cand[0]
    for i in range(0, len(rest), K):
        top = _merge_top(top, _bitonic_sort_desc(rest[i:i + K]))
    thr = top[K - 1]

    gt = [_count([c > thr for c in row]) for row in cand]
    eq = [_count([c == thr for c in row]) for row in cand]
    total_gt = functools.reduce(lambda a, b: a + b, gt)
    room = float(K) - total_gt
    n = []
    for i in range(K):
        n.append(gt[i] + jnp.clip(room, 0.0, eq[i]))
        room = room - eq[i]

    e1 = [jnp.exp(a - A[0]) for a in A]
    e2 = [jnp.exp(b - B[0]) for b in B]
    z = jnp.zeros_like(A[0])
    for i in range(K):
        inner = jnp.zeros_like(z)
        for j in range(reach[i]):
            inner = inner + jnp.where(n[i] > float(j), e2[j], 0.0)
        z = z + e1[i] * inner
    inv_z = 1.0 / z

    pinf = jnp.full_like(A[0], jnp.inf)
    t = []
    for m in range(1, K + 1):
        tm_ = pinf
        for i in range(K):
            tm_ = jnp.minimum(tm_, jnp.where(n[i] >= float(m), A[i], pinf))
        t.append(tm_)

    b_up = B[::-1]
    for k in range(nk):
        om_ref[0, keyrows(k), :] = _count_reached(s1[k], t)
        om_ref[1, keyrows(k), :] = jnp.exp(s1[k] - A[0]) * inv_z
        om_ref[2, keyrows(k), :] = float(K) - _count_reached(s2[k], b_up)
        om_ref[3, keyrows(k), :] = jnp.exp(s2[k] - B[0])

    for c in range(chunks):
        cols = slice(c * LANES, (c + 1) * LANES)
        ne_ref[0, c] = _twin_bf16(om_ref[0, pl.ds(c, nk, stride=chunks), :])
        ne_ref[1, c] = _twin_bf16(om_ref[1, pl.ds(c, nk, stride=chunks), :])
        r2_ref[:, cols] = _as_words(om_ref[2, pl.ds(c, nk, stride=chunks), :])
        e2_ref[:, cols] = _as_words(om_ref[3, pl.ds(c, nk, stride=chunks), :])


def _peer_gates(x2, g, w_q, sub_keys):
    T, D = x2.shape
    tg = VREG_TOKENS
    heads, _, nk, half_dim = sub_keys.shape
    wq_t = w_q.T.astype(BF16)
    return pl.pallas_call(
        functools.partial(_peer_gate_kernel, tg=tg),
        grid=(T // tg, heads),
        in_specs=[pl.BlockSpec((tg, D), lambda i, h: (i, 0)),
                  pl.BlockSpec((1, D), lambda i, h: (0, 0)),
                  pl.BlockSpec((2 * half_dim, D), lambda i, h: (h, 0)),
                  pl.BlockSpec((None, 2, nk, half_dim), lambda i, h: (h, 0, 0, 0))],
        out_specs=[pl.BlockSpec((D // 2, tg), lambda i, h: (0, i)),
                   pl.BlockSpec((nk // 2, tg), lambda i, h: (h, i)),
                   pl.BlockSpec((nk // 2, tg), lambda i, h: (h, i)),
                   pl.BlockSpec((None, 2, tg // LANES, nk, LANES), lambda i, h: (h, 0, i, 0, 0))],
        out_shape=[jax.ShapeDtypeStruct((D // 2, T), U32),
                   jax.ShapeDtypeStruct((heads * nk // 2, T), U32),
                   jax.ShapeDtypeStruct((heads * nk // 2, T), U32),
                   jax.ShapeDtypeStruct((heads, 2, T // LANES, nk, LANES), U32)],
        scratch_shapes=[pltpu.VMEM((2, nk * SUBLANES, LANES), F32),
                        pltpu.VMEM((4, nk * SUBLANES, LANES), F32)],
        compiler_params=_params(("parallel", "arbitrary")),
    )(x2, g.reshape(1, D), wq_t, sub_keys.astype(BF16))


KEY_BLOCK = 2
PIECE_ROWS = 256


def _gate_hidden(hid_ref, ga_ref, r2_ref, e2_ref, ne_ref, first_key, c, al0):
    nk = PEER_N_KEYS
    tiles = range(0, nk, BF16_ROWS)
    cols = slice(c * LANES, (c + 1) * LANES)

    def row_tile(h, q, al):
        return _as_bf16(ne_ref[h, q, c, pl.ds(first_key + al, SUBLANES, stride=0), :])

    gates = [[None for _ in tiles] for _ in range(KEY_BLOCK)]
    for h in range(PEER_HEADS):
        counts = [row_tile(h, 0, al0 + k) for k in range(KEY_BLOCK)]
        factors = [row_tile(h, 1, al0 + k) for k in range(KEY_BLOCK)]
        for ti, r0 in enumerate(tiles):
            hrows = slice((h * nk + r0) // 2, (h * nk + r0) // 2 + SUBLANES)
            rank, e2 = _as_bf16(r2_ref[hrows, cols]), _as_bf16(e2_ref[hrows, cols])
            for k in range(KEY_BLOCK):
                term = jnp.where(rank < counts[k], e2, jnp.zeros_like(e2)) * factors[k]
                gates[k][ti] = term if gates[k][ti] is None else gates[k][ti] + term
    for k in range(KEY_BLOCK):
        for ti, r0 in enumerate(tiles):
            rows = slice((al0 + k) * nk + r0, (al0 + k) * nk + r0 + BF16_ROWS)
            ga_ref[rows, cols] = gates[k][ti] * _gelu(hid_ref[rows, cols].astype(BF16))


def _peer_expert_kernel(ht_ref, wd_ref, wu_ref, r2_ref, e2_ref, ne_ref, x_ref, gf_ref, out_ref,
                        acc_ref, hid0_ref, hid1_ref, ga0_ref, ga1_ref, *, final_norm, n_blocks, n_work):
    s = pl.program_id(0)
    eb = hid0_ref.shape[0]
    gate_item = jnp.clip(s - 1, 0, n_work - 1)
    up_block = jnp.clip(s - 2, 0, n_work - 1) % n_blocks
    first_key = (gate_item % n_blocks) * (eb // PEER_N_KEYS)

    @pl.when(s == 0)
    def _():
        hid1_ref[...] = jnp.zeros_like(hid1_ref)
        ga0_ref[...] = jnp.zeros_like(ga0_ref)

    @pl.when(up_block == 0)
    def _():
        acc_ref[...] = jnp.zeros_like(acc_ref)

    def stages(hid_new, hid_old, ga_new, ga_old):
        tm = hid_new.shape[1]
        d_model = acc_ref.shape[0]

        def up(n, m):
            cols = slice(n * MXU_WIDTH, (n + 1) * MXU_WIDTH)
            rows = slice(m * PIECE_ROWS, (m + 1) * PIECE_ROWS)
            words = slice(m * PIECE_ROWS // 2, (m + 1) * PIECE_ROWS // 2)
            acc_ref[rows, cols] += jnp.dot(_as_bf16(wu_ref[words, :]), ga_old[:, cols],
                                           preferred_element_type=F32)

        def down(n, m):
            cols = slice(n * MXU_WIDTH, (n + 1) * MXU_WIDTH)
            rows = slice(m * PIECE_ROWS, (m + 1) * PIECE_ROWS)
            words = slice(m * PIECE_ROWS // 2, (m + 1) * PIECE_ROWS // 2)
            hid_new[rows, cols] = jnp.dot(_as_bf16(wd_ref[words, :]), _as_bf16(ht_ref[:, cols]),
                                          preferred_element_type=F32)

        mxu = [functools.partial(f, n, m) for n in range(tm // MXU_WIDTH)
               for f, size in ((up, d_model), (down, eb)) for m in range(size // PIECE_ROWS)]
        vpu = [functools.partial(_gate_hidden, hid_old, ga_new, r2_ref, e2_ref, ne_ref, first_key, c, al0)
               for c in range(tm // LANES) for al0 in range(0, eb // PEER_N_KEYS, KEY_BLOCK)]
        done = 0
        for i, piece in enumerate(mxu):
            piece()
            upto = (i + 1) * len(vpu) // len(mxu)
            for g in vpu[done:upto]:
                g()
            done = upto

    pl.when(s % 2 == 0)(lambda: stages(hid0_ref, hid1_ref, ga1_ref, ga0_ref))
    pl.when(s % 2 == 1)(lambda: stages(hid1_ref, hid0_ref, ga0_ref, ga1_ref))

    @pl.when((s >= 2) & (up_block == n_blocks - 1))
    def _():
        y = x_ref[...] + acc_ref[...].T
        out_ref[...] = _rms(y, gf_ref[...]) if final_norm else y


def _peer_experts(ht, r2, e2, ne, x2, wd_words, wu_words, layer, g_final, final_norm, tm, eb):
    T, D = x2.shape
    n_exp = wu_words.shape[2]
    heads, _, _, nk, _ = ne.shape
    n_blocks = n_exp // eb
    n_work = (T // tm) * n_blocks

    def item(lag):
        def split(s):
            w = jnp.clip(s - lag, 0, n_work - 1)
            return w // n_blocks, w % n_blocks
        return split

    down, gate, up = item(0), item(1), item(2)
    return pl.pallas_call(
        functools.partial(_peer_expert_kernel, final_norm=final_norm, n_blocks=n_blocks, n_work=n_work),
        grid=(n_work + 2,),
        in_specs=[pl.BlockSpec((D // 2, tm), lambda s: (0, down(s)[0])),
                  pl.BlockSpec((None, eb // 2, D), lambda s: (layer, down(s)[1], 0)),
                  pl.BlockSpec((None, D // 2, eb), lambda s: (layer, 0, up(s)[1])),
                  pl.BlockSpec((heads * nk // 2, tm), lambda s: (0, gate(s)[0])),
                  pl.BlockSpec((heads * nk // 2, tm), lambda s: (0, gate(s)[0])),
                  pl.BlockSpec((heads, 2, tm // LANES, nk, LANES), lambda s: (0, 0, gate(s)[0], 0, 0)),
                  pl.BlockSpec((tm, D), lambda s: (up(s)[0], 0)),
                  pl.BlockSpec((1, D), lambda s: (0, 0))],
        out_specs=pl.BlockSpec((tm, D), lambda s: (up(s)[0], 0)),
        out_shape=jax.ShapeDtypeStruct((T, D), F32),
        scratch_shapes=[pltpu.VMEM((D, tm), F32), pltpu.VMEM((eb, tm), F32), pltpu.VMEM((eb, tm), F32),
                        pltpu.VMEM((eb, tm), BF16), pltpu.VMEM((eb, tm), BF16)],
        compiler_params=_params(("arbitrary",)),
    )(ht, wd_words, wu_words, r2, e2, ne, x2, g_final.reshape(1, D))


def kernel(x, norm_mix, norm_ffn, norm_final, ab_w_in, ab_w_out, pool_w, pool_scale, c_w_in,
           c_norm, c_w_s, c_b_s, c_w_out, peer_w_q, peer_sub_keys, peer_w_down, peer_w_up):
    B, S, D = x.shape
    depth = norm_mix.shape[0]
    tm = 512
    eb = 2048
    assert S % tm == 0 and (B * S) % VREG_TOKENS == 0
    assert all(S % (d * QUERY_BLOCK) == 0 for d in DILATIONS)
    x2 = x.reshape(B * S, D)
    wd_words = _pack_rows(peer_w_down, transpose=False)
    wu_words = _pack_rows(peer_w_up, transpose=True)
    for layer in range(depth):
        j = layer // 2
        if layer % 2 == 0:
            qkv, p_in = _inproj(x2, norm_mix[layer], ab_w_in[j], S, tm)
            a_out = _attention(qkv, B, S)
            x2 = _mix0(a_out, p_in, x2, ab_w_out[j], pool_w[j], pool_scale[j], S, tm)
        else:
            x2 = _sgu(x2, norm_mix[layer], c_w_in[j], c_norm[j], c_w_s[j], c_b_s[j], c_w_out[j], tm)
        ht, r2, e2, ne = _peer_gates(x2, norm_ffn[layer], peer_w_q[layer], peer_sub_keys[layer])
        x2 = _peer_experts(ht, r2, e2, ne, x2, wd_words, wu_words, layer, norm_final,
                           layer == depth - 1, tm, eb)
    return x2.reshape(B, S, D)
```

```python
import functools
import math

import jax
import jax.numpy as jnp
import numpy as np
from jax import lax
from jax.experimental import pallas as pl
from jax.experimental.pallas import tpu as pltpu

F32 = jnp.float32
BF16 = jnp.bfloat16

LANES = 128
SUBLANES = 8
BF16_ROWS = 2 * SUBLANES
MXU_WIDTH = 256
VREG_TOKENS = SUBLANES * LANES
VMEM_LIMIT = 56 * 1024 * 1024

ATTN_HEADS = 8
HEAD_DIM = 64
ATTN_WIDTH = ATTN_HEADS * HEAD_DIM
DILATIONS = (1, 4, 16)
ATTN_HALF = 64
QUERY_BLOCK = 128
BLOCK_UNROLL = 8
ROPE_THETA = 500000.0
ROPE_DIM = HEAD_DIM // 4
POOL_WINDOWS = (2, 4, 8, 16)
POOL_HALO = 8
SGU_CHUNK = 128
SGU_GROUPS = 8
PEER_HEADS = 8
PEER_N_KEYS = 128
PEER_TOPK = 16
RMS_EPS = 1e-6
NEG_INF = -1e30
GELU_C = math.sqrt(2.0 / math.pi)


def _params(semantics):
    return pltpu.CompilerParams(dimension_semantics=semantics, vmem_limit_bytes=VMEM_LIMIT)


def _rms(x, g):
    return x * lax.rsqrt(jnp.mean(x * x, axis=-1, keepdims=True) + RMS_EPS) * g


U32 = jnp.uint32


def _as_bf16(words):
    return pltpu.bitcast(words, BF16)


def _as_words(x):
    return pltpu.bitcast(x.astype(BF16), U32)


def _pack_kernel(w_ref, o_ref, *, transpose):
    w = w_ref[...]
    o_ref[...] = _as_words(w.T if transpose else w)


def _pack_rows(w, transpose, rows=512):
    N, R, C = w.shape
    out_block, out_map = ((None, C // 2, rows), lambda n, i: (n, 0, i)) if transpose else (
        (None, rows // 2, C), lambda n, i: (n, i, 0))
    out_shape = (N, C // 2, R) if transpose else (N, R // 2, C)
    return pl.pallas_call(
        functools.partial(_pack_kernel, transpose=transpose),
        grid=(N, R // rows),
        in_specs=[pl.BlockSpec((None, rows, C), lambda n, i: (n, i, 0))],
        out_specs=pl.BlockSpec(out_block, out_map),
        out_shape=jax.ShapeDtypeStruct(out_shape, U32),
        compiler_params=_params(("parallel", "parallel")),
    )(w)


def _gelu(x):
    half = 0.5 * x
    return half + half * jnp.tanh(x * (GELU_C + (GELU_C * 0.044715) * (x * x)))


def _inproj_kernel(x_ref, g_ref, w_ref, cos_ref, sa_ref, sb_ref, qkv_ref, p_ref):
    h = _rms(x_ref[...], g_ref[...]).astype(BF16)
    proj = jnp.dot(h, w_ref[...], preferred_element_type=F32)
    c, sa, sb = cos_ref[...], sa_ref[...], sb_ref[...]
    half = ROPE_DIM // 2
    for slab in range(2 * ATTN_WIDTH // LANES):
        t = proj[:, slab * LANES:(slab + 1) * LANES]
        up = pltpu.roll(t, LANES - half, axis=1)
        dn = pltpu.roll(t, half, axis=1)
        qkv_ref[slab] = t * c + up * sa + dn * sb
    for slab in range(2 * ATTN_WIDTH // LANES, 3 * ATTN_WIDTH // LANES):
        qkv_ref[slab] = proj[:, slab * LANES:(slab + 1) * LANES]
    p_ref[...] = proj[:, 3 * ATTN_WIDTH:]


def _rope_tables(seq):
    half = ROPE_DIM // 2
    pos = jnp.arange(seq, dtype=F32)
    inv_freq = ROPE_THETA ** (-jnp.arange(0, ROPE_DIM, 2, dtype=F32) / ROPE_DIM)
    ang = pos[:, None] * inv_freq[None, :]
    cos, sin = jnp.cos(ang), jnp.sin(ang)
    pad = HEAD_DIM - ROPE_DIM
    ones = jnp.ones((seq, pad), F32)
    zeros = jnp.zeros((seq, pad), F32)
    zh = jnp.zeros((seq, half), F32)
    c = jnp.concatenate([cos, cos, ones], axis=1)
    sa = jnp.concatenate([-sin, zh, zeros], axis=1)
    sb = jnp.concatenate([zh, sin, zeros], axis=1)
    return tuple(jnp.concatenate([t, t], axis=1) for t in (c, sa, sb))


def _inproj(x2, g, w_in, seq, tm):
    T, D = x2.shape
    n_out = w_in.shape[1]
    pool_w = n_out - 3 * ATTN_WIDTH
    c, sa, sb = _rope_tables(seq)
    spt = seq // tm
    tab = pl.BlockSpec((tm, LANES), lambda i: (i % spt, 0))
    return pl.pallas_call(
        _inproj_kernel,
        grid=(T // tm,),
        in_specs=[pl.BlockSpec((tm, D), lambda i: (i, 0)),
                  pl.BlockSpec((1, D), lambda i: (0, 0)),
                  pl.BlockSpec((D, n_out), lambda i: (0, 0)),
                  tab, tab, tab],
        out_specs=[pl.BlockSpec((3 * ATTN_WIDTH // LANES, tm, LANES), lambda i: (0, i, 0)),
                   pl.BlockSpec((tm, pool_w), lambda i: (i, 0))],
        out_shape=[jax.ShapeDtypeStruct((3 * ATTN_WIDTH // LANES, T, LANES), F32),
                   jax.ShapeDtypeStruct((T, pool_w), F32)],
        compiler_params=_params(("parallel",)),
    )(x2, g.reshape(1, D), w_in.astype(BF16), c, sa, sb)


def _attn_kernel(q_ref, k_ref, v_ref, a_ref, o_scr, lse_scr, *, seq):
    scale = HEAD_DIM ** -0.5
    lane = lax.broadcasted_iota(jnp.int32, (QUERY_BLOCK, LANES), 1)
    first_head = lane < HEAD_DIM
    head_masks = (first_head, jnp.logical_not(first_head))

    for pat, dil in enumerate(DILATIONS):
        length = seq // dil
        kw = min(QUERY_BLOCK + 2 * ATTN_HALF, length)
        blocks = length // QUERY_BLOCK
        ones = jnp.ones((kw, LANES), BF16)

        def group(g, carry, dil=dil, length=length, kw=kw, blocks=blocks, pat=pat, ones=ones):
            loaded = []
            for j in range(BLOCK_UNROLL):
                it = g * BLOCK_UNROLL + j
                res, qb = it // blocks, it % blocks
                q0 = qb * QUERY_BLOCK
                start = jnp.clip(q0 - ATTN_HALF, 0, length - kw)

                def rows(ref, first, count, res=res):
                    return ref[pl.ds(res + dil * first, count, stride=dil), :]

                qpos = q0 + lax.broadcasted_iota(jnp.int32, (QUERY_BLOCK, kw), 0)
                kpos = start + lax.broadcasted_iota(jnp.int32, (QUERY_BLOCK, kw), 1)
                loaded.append((rows(q_ref, q0, QUERY_BLOCK).astype(BF16), rows(k_ref, start, kw).astype(BF16),
                               rows(v_ref, start, kw).astype(BF16), jnp.abs(kpos - qpos) <= ATTN_HALF,
                               pl.ds(res + dil * q0, QUERY_BLOCK, stride=dil)))
            chains = [(j, mask) for j in range(BLOCK_UNROLL) for mask in head_masks]
            scores = []
            for j, mask in chains:
                qp, kp, _, valid, _ = loaded[j]
                qm = jnp.where(mask, qp, jnp.zeros_like(qp))
                s = lax.dot_general(qm, kp, (((1,), (1,)), ((), ())), preferred_element_type=F32) * scale
                scores.append(jnp.where(valid, s, NEG_INF))
            tops = [jnp.max(s, axis=-1, keepdims=True) for s in scores]
            probs = [jnp.exp(s - m).astype(BF16) for s, m in zip(scores, tops)]
            totals = [jnp.dot(p, ones, preferred_element_type=F32) for p in probs]
            mixes = [jnp.dot(p, loaded[j][2], preferred_element_type=F32) for p, (j, _) in zip(probs, chains)]
            outs = [o / t for o, t in zip(mixes, totals)]
            lses = [m + jnp.log(t) for m, t in zip(tops, totals)]
            for j in range(BLOCK_UNROLL):
                dst = loaded[j][4]
                o_scr[pat, dst, :] = jnp.where(first_head, outs[2 * j], outs[2 * j + 1])
                lse_scr[pat, dst, :] = jnp.where(first_head, lses[2 * j], lses[2 * j + 1])
            return carry

        assert (dil * blocks) % BLOCK_UNROLL == 0
        lax.fori_loop(0, dil * blocks // BLOCK_UNROLL, group, 0)

    la, lb, lc = lse_scr[0], lse_scr[1], lse_scr[2]
    m = jnp.maximum(jnp.maximum(la, lb), lc)
    ea, eb, ec = jnp.exp(la - m), jnp.exp(lb - m), jnp.exp(lc - m)
    a_ref[...] = ((o_scr[0] * ea + o_scr[1] * eb + o_scr[2] * ec) / (ea + eb + ec)).astype(BF16)


def _attention(qkv, batch, seq):
    pairs = ATTN_WIDTH // LANES

    def part(j):
        return pl.BlockSpec((None, seq, LANES), lambda b, hp: (j * pairs + hp, b, 0))

    return pl.pallas_call(
        functools.partial(_attn_kernel, seq=seq),
        grid=(batch, pairs),
        in_specs=[part(0), part(1), part(2)],
        out_specs=pl.BlockSpec((seq, LANES), lambda b, hp: (b, hp)),
        out_shape=jax.ShapeDtypeStruct((batch * seq, ATTN_WIDTH), BF16),
        scratch_shapes=[pltpu.VMEM((len(DILATIONS), seq, LANES), F32),
                        pltpu.VMEM((len(DILATIONS), seq, LANES), F32)],
        compiler_params=_params(("parallel", "parallel")),
    )(qkv, qkv, qkv)


def _mix0_kernel(a_ref, pc_ref, pp_ref, pn_ref, x_ref, wout_ref, pw_ref, ps_ref,
                 out_ref, ext_ref, *, seq, tm):
    ti = pl.program_id(0) % (seq // tm)
    acc = jnp.dot(a_ref[...], wout_ref[:ATTN_WIDTH, :], preferred_element_type=F32)

    ext_ref[:POOL_HALO, :] = jnp.where(ti > 0, pp_ref[...], 0.0)
    ext_ref[POOL_HALO:POOL_HALO + tm, :] = pc_ref[...]
    ext_ref[POOL_HALO + tm:, :] = jnp.where(ti < seq // tm - 1, pn_ref[...], 0.0)
    pos = ti * tm + lax.broadcasted_iota(jnp.int32, (tm, 1), 0)
    for g, win in enumerate(POOL_WINDOWS):
        cols = slice(g * LANES, (g + 1) * LANES)
        before, after = win // 2, win - win // 2
        tot = ext_ref[POOL_HALO - before:POOL_HALO - before + tm, cols]
        for d in range(-before + 1, after):
            tot = tot + ext_ref[POOL_HALO + d:POOL_HALO + d + tm, cols]
        cnt = (jnp.minimum(pos + after, seq) - jnp.maximum(pos - before, 0)).astype(F32)
        diff = tot / cnt - pc_ref[:, cols]
        mixed = jnp.dot(diff.astype(BF16), pw_ref[g], preferred_element_type=F32) * ps_ref[:, cols]
        acc = acc + jnp.dot(mixed.astype(BF16),
                            wout_ref[ATTN_WIDTH + g * LANES:ATTN_WIDTH + (g + 1) * LANES, :],
                            preferred_element_type=F32)
    out_ref[...] = x_ref[...] + acc


def _mix0(a_out, p_in, x2, w_out, pool_w, pool_scale, seq, tm):
    T, D = x2.shape
    pw = p_in.shape[1]
    hb = tm // POOL_HALO
    last = T // POOL_HALO - 1
    row = lambda w: pl.BlockSpec((tm, w), lambda i: (i, 0))
    full = lambda s: pl.BlockSpec(s, lambda i: (0,) * len(s))
    return pl.pallas_call(
        functools.partial(_mix0_kernel, seq=seq, tm=tm),
        grid=(T // tm,),
        in_specs=[
            row(ATTN_WIDTH), row(pw),
            pl.BlockSpec((POOL_HALO, pw), lambda i: (jnp.maximum(i * hb - 1, 0), 0)),
            pl.BlockSpec((POOL_HALO, pw), lambda i: (jnp.minimum((i + 1) * hb, last), 0)),
            row(D), full(w_out.shape), full(pool_w.shape), full((1, pw))],
        out_specs=row(D),
        out_shape=jax.ShapeDtypeStruct((T, D), F32),
        scratch_shapes=[pltpu.VMEM((tm + 2 * POOL_HALO, pw), F32)],
        compiler_params=_params(("parallel",)),
    )(a_out, p_in, p_in, p_in, x2, w_out.astype(BF16), pool_w.astype(BF16),
      pool_scale.reshape(1, pw))


def _sgu_kernel(x_ref, g_ref, win_ref, cn_ref, ws_ref, bs_ref, wout_ref, out_ref, gated_ref, *, tm):
    x = x_ref[...]
    width = wout_ref.shape[0]
    h = _rms(x, g_ref[...]).astype(BF16)
    u = _gelu(jnp.dot(h, win_ref[:, :width], preferred_element_type=F32))
    v = _gelu(jnp.dot(h, win_ref[:, width:], preferred_element_type=F32))
    vb = _rms(v, cn_ref[...]).astype(BF16)
    for n in range(tm // SGU_CHUNK):
        rows = slice(n * SGU_CHUNK, (n + 1) * SGU_CHUNK)
        for g in range(SGU_GROUPS):
            cols = slice(g * LANES, (g + 1) * LANES)
            mixed = jnp.dot(ws_ref[g], vb[rows, cols], preferred_element_type=F32) + bs_ref[:, cols]
            gated_ref[rows, cols] = (u[rows, cols] * mixed).astype(BF16)
    out_ref[...] = x + jnp.dot(gated_ref[...], wout_ref[...], preferred_element_type=F32)


def _sgu(x2, g, w_in, c_norm, w_s, b_s, w_out, tm):
    T, D = x2.shape
    width = w_out.shape[0]
    bias = jnp.repeat(b_s.T, width // SGU_GROUPS, axis=1)
    full = lambda s: pl.BlockSpec(s, lambda i: (0,) * len(s))
    return pl.pallas_call(
        functools.partial(_sgu_kernel, tm=tm),
        grid=(T // tm,),
        in_specs=[pl.BlockSpec((tm, D), lambda i: (i, 0)), full((1, D)), full(w_in.shape),
                  full((1, width)), full(w_s.shape), full(bias.shape), full(w_out.shape)],
        out_specs=pl.BlockSpec((tm, D), lambda i: (i, 0)),
        out_shape=jax.ShapeDtypeStruct((T, D), F32),
        scratch_shapes=[pltpu.VMEM((tm, width), BF16)],
        compiler_params=_params(("parallel",)),
    )(x2, g.reshape(1, D), w_in.astype(BF16), c_norm.reshape(1, width), w_s.astype(BF16), bias,
      w_out.astype(BF16))


def _cmpx(v, i, l, descending):
    hi, lo = jnp.maximum(v[i], v[l]), jnp.minimum(v[i], v[l])
    v[i], v[l] = (hi, lo) if descending else (lo, hi)


def _bitonic_sort_desc(v):
    v = list(v)
    n = len(v)
    k = 2
    while k <= n:
        j = k // 2
        while j >= 1:
            for i in range(n):
                l = i ^ j
                if l > i:
                    _cmpx(v, i, l, (i & k) == 0)
            j //= 2
        k *= 2
    return v


def _merge_top(a, b):
    n = len(a)
    v = [jnp.maximum(a[i], b[n - 1 - i]) for i in range(n)]
    j = n // 2
    while j >= 1:
        for i in range(n):
            if (i & j) == 0:
                _cmpx(v, i, i + j, True)
        j //= 2
    return v


def _top_sorted(vals, k):
    groups = [_bitonic_sort_desc(vals[i:i + k]) for i in range(0, len(vals), k)]
    while len(groups) > 1:
        groups = [_merge_top(groups[i], groups[i + 1]) for i in range(0, len(groups), 2)]
    return groups[0]


def _count_reached(s, t):
    assert len(t) == 16
    c16 = s >= t[15]
    c8 = s >= t[7]
    c4 = s >= jnp.where(c8, t[11], t[3])
    c2 = s >= jnp.where(c8, jnp.where(c4, t[13], t[9]), jnp.where(c4, t[5], t[1]))
    quarter = [jnp.where(c2, t[4 * q + 2], t[4 * q]) for q in range(4)]
    c1 = s >= jnp.where(c8, jnp.where(c4, quarter[3], quarter[2]), jnp.where(c4, quarter[1], quarter[0]))
    return (jnp.where(c8, 8.0, 0.0) + jnp.where(c4, 4.0, 0.0) + jnp.where(c2, 2.0, 0.0)
            + jnp.where(c1, 1.0, 0.0) + jnp.where(c16, 1.0, 0.0))


def _count(preds):
    tot = jnp.where(preds[0], 1.0, 0.0)
    for p in preds[1:]:
        tot = tot + jnp.where(p, 1.0, 0.0)
    return tot


def _twin_bf16(v):
    bits = pltpu.bitcast(v.astype(BF16).astype(F32), U32)
    return bits | (bits >> 16)


def _peer_gate_kernel(x_ref, g_ref, wq_ref, keys_ref, ht_ref, r2_ref, e2_ref, ne_ref,
                      km_ref, om_ref, *, tg):
    K = PEER_TOPK
    nk = PEER_N_KEYS
    chunks = tg // LANES

    @pl.when(pl.program_id(1) == 0)
    def _():
        ht_ref[...] = _as_words(_rms(x_ref[...], g_ref[...]).T)

    qt = jnp.dot(wq_ref[...], _as_bf16(ht_ref[...]), preferred_element_type=F32)
    half_dim = qt.shape[0] // 2
    for side in range(2):
        sc = jnp.dot(keys_ref[side], qt[side * half_dim:(side + 1) * half_dim].astype(BF16),
                     preferred_element_type=F32)
        for c in range(chunks):
            km_ref[side, pl.ds(c, nk, stride=chunks), :] = sc[:, c * LANES:(c + 1) * LANES]

    def keyrows(k):
        return slice(k * chunks, (k + 1) * chunks)

    s1 = [km_ref[0, keyrows(k), :] for k in range(nk)]
    s2 = [km_ref[1, keyrows(k), :] for k in range(nk)]
    A = _top_sorted(s1, K)
    B = _top_sorted(s2, K)

    reach = [K // (i + 1) for i in range(K)]
    cand = [[A[i] + B[j] for j in range(reach[i])] for i in range(K)]
    ninf = jnp.full_like(A[0], -jnp.inf)
    rest = [c for row in cand[1:] for c in row]
    rest = rest + [ninf] * (-len(rest) % K)
    top = cand[0]
    for i in range(0, len(rest), K):
        top = _merge_top(top, _bitonic_sort_desc(rest[i:i + K]))
    thr = top[K - 1]

    gt = [_count([c > thr for c in row]) for row in cand]
    eq = [_count([c == thr for c in row]) for row in cand]
    total_gt = functools.reduce(lambda a, b: a + b, gt)
    room = float(K) - total_gt
    n = []
    for i in range(K):
        n.append(gt[i] + jnp.clip(room, 0.0, eq[i]))
        room = room - eq[i]

    e1 = [jnp.exp(a - A[0]) for a in A]
    e2 = [jnp.exp(b - B[0]) for b in B]
    z = jnp.zeros_like(A[0])
    for i in range(K):
        inner = jnp.zeros_like(z)
        for j in range(reach[i]):
            inner = inner + jnp.where(n[i] > float(j), e2[j], 0.0)
        z = z + e1[i] * inner
    inv_z = 1.0 / z

    pinf = jnp.full_like(A[0], jnp.inf)
    t = []
    for m in range(1, K + 1):
        tm_ = pinf
        for i in range(K):
            tm_ = jnp.minimum(tm_, jnp.where(n[i] >= float(m), A[i], pinf))
        t.append(tm_)

    b_up = B[::-1]
    for k in range(nk):
        om_ref[0, keyrows(k), :] = _count_reached(s1[k], t)
        om_ref[1, keyrows(k), :] = jnp.exp(s1[k] - A[0]) * inv_z
        om_ref[2, keyrows(k), :] = float(K) - _count_reached(s2[k], b_up)
        om_ref[3, keyrows(k), :] = jnp.exp(s2[k] - B[0])

    for c in range(chunks):
        cols = slice(c * LANES, (c + 1) * LANES)
        ne_ref[0, c] = _twin_bf16(om_ref[0, pl.ds(c, nk, stride=chunks), :])
        ne_ref[1, c] = _twin_bf16(om_ref[1, pl.ds(c, nk, stride=chunks), :])
        r2_ref[:, cols] = _as_words(om_ref[2, pl.ds(c, nk, stride=chunks), :])
        e2_ref[:, cols] = _as_words(om_ref[3, pl.ds(c, nk, stride=chunks), :])


def _peer_gates(x2, g, w_q, sub_keys):
    T, D = x2.shape
    tg = VREG_TOKENS
    heads, _, nk, half_dim = sub_keys.shape
    wq_t = w_q.T.astype(BF16)
    return pl.pallas_call(
        functools.partial(_peer_gate_kernel, tg=tg),
        grid=(T // tg, heads),
        in_specs=[pl.BlockSpec((tg, D), lambda i, h: (i, 0)),
                  pl.BlockSpec((1, D), lambda i, h: (0, 0)),
                  pl.BlockSpec((2 * half_dim, D), lambda i, h: (h, 0)),
                  pl.BlockSpec((None, 2, nk, half_dim), lambda i, h: (h, 0, 0, 0))],
        out_specs=[pl.BlockSpec((D // 2, tg), lambda i, h: (0, i)),
                   pl.BlockSpec((nk // 2, tg), lambda i, h: (h, i)),
                   pl.BlockSpec((nk // 2, tg), lambda i, h: (h, i)),
                   pl.BlockSpec((None, 2, tg // LANES, nk, LANES), lambda i, h: (h, 0, i, 0, 0))],
        out_shape=[jax.ShapeDtypeStruct((D // 2, T), U32),
                   jax.ShapeDtypeStruct((heads * nk // 2, T), U32),
                   jax.ShapeDtypeStruct((heads * nk // 2, T), U32),
                   jax.ShapeDtypeStruct((heads, 2, T // LANES, nk, LANES), U32)],
        scratch_shapes=[pltpu.VMEM((2, nk * SUBLANES, LANES), F32),
                        pltpu.VMEM((4, nk * SUBLANES, LANES), F32)],
        compiler_params=_params(("parallel", "arbitrary")),
    )(x2, g.reshape(1, D), wq_t, sub_keys.astype(BF16))


KEY_BLOCK = 4
PIECE_ROWS = 256


def _gate_hidden(hid_ref, ga_ref, r2_ref, e2_ref, ne_ref, first_key, c, al0):
    nk = PEER_N_KEYS
    tiles = range(0, nk, BF16_ROWS)
    cols = slice(c * LANES, (c + 1) * LANES)

    def row_tile(h, q, al):
        return _as_bf16(ne_ref[h, q, c, pl.ds(first_key + al, SUBLANES, stride=0), :])

    gates = [[None for _ in tiles] for _ in range(KEY_BLOCK)]
    for h in range(PEER_HEADS):
        counts = [row_tile(h, 0, al0 + k) for k in range(KEY_BLOCK)]
        factors = [row_tile(h, 1, al0 + k) for k in range(KEY_BLOCK)]
        for ti, r0 in enumerate(tiles):
            hrows = slice((h * nk + r0) // 2, (h * nk + r0) // 2 + SUBLANES)
            rank, e2 = _as_bf16(r2_ref[hrows, cols]), _as_bf16(e2_ref[hrows, cols])
            for k in range(KEY_BLOCK):
                term = jnp.where(rank < counts[k], e2, jnp.zeros_like(e2)) * factors[k]
                gates[k][ti] = term if gates[k][ti] is None else gates[k][ti] + term
    for k in range(KEY_BLOCK):
        for ti, r0 in enumerate(tiles):
            rows = slice((al0 + k) * nk + r0, (al0 + k) * nk + r0 + BF16_ROWS)
            ga_ref[rows, cols] = gates[k][ti] * _gelu(hid_ref[rows, cols].astype(BF16))


def _peer_expert_kernel(ht_ref, wd_ref, wu_ref, r2_ref, e2_ref, ne_ref, x_ref, gf_ref, out_ref,
                        acc_ref, hid0_ref, hid1_ref, ga0_ref, ga1_ref, *, final_norm, n_blocks, n_work):
    s = pl.program_id(0)
    eb = hid0_ref.shape[0]
    gate_item = jnp.clip(s - 1, 0, n_work - 1)
    up_block = jnp.clip(s - 2, 0, n_work - 1) % n_blocks
    first_key = (gate_item % n_blocks) * (eb // PEER_N_KEYS)

    @pl.when(s == 0)
    def _():
        hid1_ref[...] = jnp.zeros_like(hid1_ref)
        ga0_ref[...] = jnp.zeros_like(ga0_ref)

    @pl.when(up_block == 0)
    def _():
        acc_ref[...] = jnp.zeros_like(acc_ref)

    def stages(hid_new, hid_old, ga_new, ga_old):
        tm = hid_new.shape[1]
        d_model = acc_ref.shape[0]

        def up(n, m):
            cols = slice(n * MXU_WIDTH, (n + 1) * MXU_WIDTH)
            rows = slice(m * PIECE_ROWS, (m + 1) * PIECE_ROWS)
            words = slice(m * PIECE_ROWS // 2, (m + 1) * PIECE_ROWS // 2)
            acc_ref[rows, cols] += jnp.dot(_as_bf16(wu_ref[words, :]), ga_old[:, cols],
                                           preferred_element_type=F32)

        def down(n, m):
            cols = slice(n * MXU_WIDTH, (n + 1) * MXU_WIDTH)
            rows = slice(m * PIECE_ROWS, (m + 1) * PIECE_ROWS)
            words = slice(m * PIECE_ROWS // 2, (m + 1) * PIECE_ROWS // 2)
            hid_new[rows, cols] = jnp.dot(_as_bf16(wd_ref[words, :]), _as_bf16(ht_ref[:, cols]),
                                          preferred_element_type=F32)

        mxu = [functools.partial(f, n, m) for n in range(tm // MXU_WIDTH)
               for f, size in ((up, d_model), (down, eb)) for m in range(size // PIECE_ROWS)]
        vpu = [functools.partial(_gate_hidden, hid_old, ga_new, r2_ref, e2_ref, ne_ref, first_key, c, al0)
               for c in range(tm // LANES) for al0 in range(0, eb // PEER_N_KEYS, KEY_BLOCK)]
        done = 0
        for i, piece in enumerate(mxu):
            piece()
            upto = (i + 1) * len(vpu) // len(mxu)
            for g in vpu[done:upto]:
                g()
            done = upto

    pl.when(s % 2 == 0)(lambda: stages(hid0_ref, hid1_ref, ga1_ref, ga0_ref))
    pl.when(s % 2 == 1)(lambda: stages(hid1_ref, hid0_ref, ga0_ref, ga1_ref))

    @pl.when((s >= 2) & (up_block == n_blocks - 1))
    def _():
        y = x_ref[...] + acc_ref[...].T
        out_ref[...] = _rms(y, gf_ref[...]) if final_norm else y


def _peer_experts(ht, r2, e2, ne, x2, wd_words, wu_words, layer, g_final, final_norm, tm, eb):
    T, D = x2.shape
    n_exp = wu_words.shape[2]
    heads, _, _, nk, _ = ne.shape
    n_blocks = n_exp // eb
    n_work = (T // tm) * n_blocks

    def item(lag):
        def split(s):
            w = jnp.clip(s - lag, 0, n_work - 1)
            return w // n_blocks, w % n_blocks
        return split

    down, gate, up = item(0), item(1), item(2)
    return pl.pallas_call(
        functools.partial(_peer_expert_kernel, final_norm=final_norm, n_blocks=n_blocks, n_work=n_work),
        grid=(n_work + 2,),
        in_specs=[pl.BlockSpec((D // 2, tm), lambda s: (0, down(s)[0])),
                  pl.BlockSpec((None, eb // 2, D), lambda s: (layer, down(s)[1], 0)),
                  pl.BlockSpec((None, D // 2, eb), lambda s: (layer, 0, up(s)[1])),
                  pl.BlockSpec((heads * nk // 2, tm), lambda s: (0, gate(s)[0])),
                  pl.BlockSpec((heads * nk // 2, tm), lambda s: (0, gate(s)[0])),
                  pl.BlockSpec((heads, 2, tm // LANES, nk, LANES), lambda s: (0, 0, gate(s)[0], 0, 0)),
                  pl.BlockSpec((tm, D), lambda s: (up(s)[0], 0)),
                  pl.BlockSpec((1, D), lambda s: (0, 0))],
        out_specs=pl.BlockSpec((tm, D), lambda s: (up(s)[0], 0)),
        out_shape=jax.ShapeDtypeStruct((T, D), F32),
        scratch_shapes=[pltpu.VMEM((D, tm), F32), pltpu.VMEM((eb, tm), F32), pltpu.VMEM((eb, tm), F32),
                        pltpu.VMEM((eb, tm), BF16), pltpu.VMEM((eb, tm), BF16)],
        compiler_params=_params(("arbitrary",)),
    )(ht, wd_words, wu_words, r2, e2, ne, x2, g_final.reshape(1, D))


def kernel(x, norm_mix, norm_ffn, norm_final, ab_w_in, ab_w_out, pool_w, pool_scale, c_w_in,
           c_norm, c_w_s, c_b_s, c_w_out, peer_w_q, peer_sub_keys, peer_w_down, peer_w_up):
    B, S, D = x.shape
    depth = norm_mix.shape[0]
    tm = 512
    eb = 2048
    assert S % tm == 0 and (B * S) % VREG_TOKENS == 0
    assert all(S % (d * QUERY_BLOCK) == 0 for d in DILATIONS)
    x2 = x.reshape(B * S, D)
    wd_words = _pack_rows(peer_w_down, transpose=False)
    wu_words = _pack_rows(peer_w_up, transpose=True)
    for layer in range(depth):
        j = layer // 2
        if layer % 2 == 0:
            qkv, p_in = _inproj(x2, norm_mix[layer], ab_w_in[j], S, tm)
            a_out = _attention(qkv, B, S)
            x2 = _mix0(a_out, p_in, x2, ab_w_out[j], pool_w[j], pool_scale[j], S, tm)
        else:
            x2 = _sgu(x2, norm_mix[layer], c_w_in[j], c_norm[j], c_w_s[j], c_b_s[j], c_w_out[j], tm)
        ht, r2, e2, ne = _peer_gates(x2, norm_ffn[layer], peer_w_q[layer], peer_sub_keys[layer])
        x2 = _peer_experts(ht, r2, e2, ne, x2, wd_words, wu_words, layer, norm_final,
                           layer == depth - 1, tm, eb)
    return x2.reshape(B, S, D)
```

```python
import functools
import math

import jax
import jax.numpy as jnp
import numpy as np
from jax import lax
from jax.experimental import pallas as pl
from jax.experimental.pallas import tpu as pltpu

F32 = jnp.float32
BF16 = jnp.bfloat16

LANES = 128
SUBLANES = 8
BF16_ROWS = 2 * SUBLANES
MXU_WIDTH = 256
VREG_TOKENS = SUBLANES * LANES
VMEM_LIMIT = 56 * 1024 * 1024

ATTN_HEADS = 8
HEAD_DIM = 64
ATTN_WIDTH = ATTN_HEADS * HEAD_DIM
DILATIONS = (1, 4, 16)
ATTN_HALF = 64
QUERY_BLOCK = 128
BLOCK_UNROLL = 8
ROPE_THETA = 500000.0
ROPE_DIM = HEAD_DIM // 4
POOL_WINDOWS = (2, 4, 8, 16)
POOL_HALO = 8
SGU_CHUNK = 128
SGU_GROUPS = 8
PEER_HEADS = 8
PEER_N_KEYS = 128
PEER_TOPK = 16
RMS_EPS = 1e-6
NEG_INF = -1e30
GELU_C = math.sqrt(2.0 / math.pi)


def _params(semantics):
    return pltpu.CompilerParams(dimension_semantics=semantics, vmem_limit_bytes=VMEM_LIMIT)


def _rms(x, g):
    return x * lax.rsqrt(jnp.mean(x * x, axis=-1, keepdims=True) + RMS_EPS) * g


U32 = jnp.uint32


def _as_bf16(words):
    return pltpu.bitcast(words, BF16)


def _as_words(x):
    return pltpu.bitcast(x.astype(BF16), U32)


def _pack_kernel(w_ref, o_ref, *, transpose):
    w = w_ref[...]
    o_ref[...] = _as_words(w.T if transpose else w)


def _pack_rows(w, transpose, rows=512):
    N, R, C = w.shape
    out_block, out_map = ((None, C // 2, rows), lambda n, i: (n, 0, i)) if transpose else (
        (None, rows // 2, C), lambda n, i: (n, i, 0))
    out_shape = (N, C // 2, R) if transpose else (N, R // 2, C)
    return pl.pallas_call(
        functools.partial(_pack_kernel, transpose=transpose),
        grid=(N, R // rows),
        in_specs=[pl.BlockSpec((None, rows, C), lambda n, i: (n, i, 0))],
        out_specs=pl.BlockSpec(out_block, out_map),
        out_shape=jax.ShapeDtypeStruct(out_shape, U32),
        compiler_params=_params(("parallel", "parallel")),
    )(w)


def _gelu(x):
    half = 0.5 * x
    return half + half * jnp.tanh(x * (GELU_C + (GELU_C * 0.044715) * (x * x)))


def _inproj_kernel(x_ref, g_ref, w_ref, cos_ref, sa_ref, sb_ref, qkv_ref, p_ref):
    h = _rms(x_ref[...], g_ref[...]).astype(BF16)
    proj = jnp.dot(h, w_ref[...], preferred_element_type=F32)
    c, sa, sb = cos_ref[...], sa_ref[...], sb_ref[...]
    half = ROPE_DIM // 2
    for slab in range(2 * ATTN_WIDTH // LANES):
        t = proj[:, slab * LANES:(slab + 1) * LANES]
        up = pltpu.roll(t, LANES - half, axis=1)
        dn = pltpu.roll(t, half, axis=1)
        qkv_ref[slab] = t * c + up * sa + dn * sb
    for slab in range(2 * ATTN_WIDTH // LANES, 3 * ATTN_WIDTH // LANES):
        qkv_ref[slab] = proj[:, slab * LANES:(slab + 1) * LANES]
    p_ref[...] = proj[:, 3 * ATTN_WIDTH:]


def _rope_tables(seq):
    half = ROPE_DIM // 2
    pos = jnp.arange(seq, dtype=F32)
    inv_freq = ROPE_THETA ** (-jnp.arange(0, ROPE_DIM, 2, dtype=F32) / ROPE_DIM)
    ang = pos[:, None] * inv_freq[None, :]
    cos, sin = jnp.cos(ang), jnp.sin(ang)
    pad = HEAD_DIM - ROPE_DIM
    ones = jnp.ones((seq, pad), F32)
    zeros = jnp.zeros((seq, pad), F32)
    zh = jnp.zeros((seq, half), F32)
    c = jnp.concatenate([cos, cos, ones], axis=1)
    sa = jnp.concatenate([-sin, zh, zeros], axis=1)
    sb = jnp.concatenate([zh, sin, zeros], axis=1)
    return tuple(jnp.concatenate([t, t], axis=1) for t in (c, sa, sb))


def _inproj(x2, g, w_in, seq, tm):
    T, D = x2.shape
    n_out = w_in.shape[1]
    pool_w = n_out - 3 * ATTN_WIDTH
    c, sa, sb = _rope_tables(seq)
    spt = seq // tm
    tab = pl.BlockSpec((tm, LANES), lambda i: (i % spt, 0))
    return pl.pallas_call(
        _inproj_kernel,
        grid=(T // tm,),
        in_specs=[pl.BlockSpec((tm, D), lambda i: (i, 0)),
                  pl.BlockSpec((1, D), lambda i: (0, 0)),
                  pl.BlockSpec((D, n_out), lambda i: (0, 0)),
                  tab, tab, tab],
        out_specs=[pl.BlockSpec((3 * ATTN_WIDTH // LANES, tm, LANES), lambda i: (0, i, 0)),
                   pl.BlockSpec((tm, pool_w), lambda i: (i, 0))],
        out_shape=[jax.ShapeDtypeStruct((3 * ATTN_WIDTH // LANES, T, LANES), F32),
                   jax.ShapeDtypeStruct((T, pool_w), F32)],
        compiler_params=_params(("parallel",)),
    )(x2, g.reshape(1, D), w_in.astype(BF16), c, sa, sb)


def _attn_kernel(q_ref, k_ref, v_ref, a_ref, o_scr, lse_scr, *, seq):
    scale = HEAD_DIM ** -0.5
    lane = lax.broadcasted_iota(jnp.int32, (QUERY_BLOCK, LANES), 1)
    first_head = lane < HEAD_DIM
    head_masks = (first_head, jnp.logical_not(first_head))

    for pat, dil in enumerate(DILATIONS):
        length = seq // dil
        kw = min(QUERY_BLOCK + 2 * ATTN_HALF, length)
        blocks = length // QUERY_BLOCK
        ones = jnp.ones((kw, LANES), BF16)

        def group(g, carry, dil=dil, length=length, kw=kw, blocks=blocks, pat=pat, ones=ones):
            loaded = []
            for j in range(BLOCK_UNROLL):
                it = g * BLOCK_UNROLL + j
                res, qb = it // blocks, it % blocks
                q0 = qb * QUERY_BLOCK
                start = jnp.clip(q0 - ATTN_HALF, 0, length - kw)

                def rows(ref, first, count, res=res):
                    return ref[pl.ds(res + dil * first, count, stride=dil), :]

                qpos = q0 + lax.broadcasted_iota(jnp.int32, (QUERY_BLOCK, kw), 0)
                kpos = start + lax.broadcasted_iota(jnp.int32, (QUERY_BLOCK, kw), 1)
                loaded.append((rows(q_ref, q0, QUERY_BLOCK).astype(BF16), rows(k_ref, start, kw).astype(BF16),
                               rows(v_ref, start, kw).astype(BF16), jnp.abs(kpos - qpos) <= ATTN_HALF,
                               pl.ds(res + dil * q0, QUERY_BLOCK, stride=dil)))
            chains = [(j, mask) for j in range(BLOCK_UNROLL) for mask in head_masks]
            scores = []
            for j, mask in chains:
                qp, kp, _, valid, _ = loaded[j]
                qm = jnp.where(mask, qp, jnp.zeros_like(qp))
                s = lax.dot_general(qm, kp, (((1,), (1,)), ((), ())), preferred_element_type=F32) * scale
                scores.append(jnp.where(valid, s, NEG_INF))
            tops = [jnp.max(s, axis=-1, keepdims=True) for s in scores]
            probs = [jnp.exp(s - m).astype(BF16) for s, m in zip(scores, tops)]
            totals = [jnp.dot(p, ones, preferred_element_type=F32) for p in probs]
            mixes = [jnp.dot(p, loaded[j][2], preferred_element_type=F32) for p, (j, _) in zip(probs, chains)]
            outs = [o / t for o, t in zip(mixes, totals)]
            lses = [m + jnp.log(t) for m, t in zip(tops, totals)]
            for j in range(BLOCK_UNROLL):
                dst = loaded[j][4]
                o_scr[pat, dst, :] = jnp.where(first_head, outs[2 * j], outs[2 * j + 1])
                lse_scr[pat, dst, :] = jnp.where(first_head, lses[2 * j], lses[2 * j + 1])
            return carry

        assert (dil * blocks) % BLOCK_UNROLL == 0
        lax.fori_loop(0, dil * blocks // BLOCK_UNROLL, group, 0)

    la, lb, lc = lse_scr[0], lse_scr[1], lse_scr[2]
    m = jnp.maximum(jnp.maximum(la, lb), lc)
    ea, eb, ec = jnp.exp(la - m), jnp.exp(lb - m), jnp.exp(lc - m)
    a_ref[...] = ((o_scr[0] * ea + o_scr[1] * eb + o_scr[2] * ec) / (ea + eb + ec)).astype(BF16)


def _attention(qkv, batch, seq):
    pairs = ATTN_WIDTH // LANES

    def part(j):
        return pl.BlockSpec((None, seq, LANES), lambda b, hp: (j * pairs + hp, b, 0))

    return pl.pallas_call(
        functools.partial(_attn_kernel, seq=seq),
        grid=(batch, pairs),
        in_specs=[part(0), part(1), part(2)],
        out_specs=pl.BlockSpec((seq, LANES), lambda b, hp: (b, hp)),
        out_shape=jax.ShapeDtypeStruct((batch * seq, ATTN_WIDTH), BF16),
        scratch_shapes=[pltpu.VMEM((len(DILATIONS), seq, LANES), F32),
                        pltpu.VMEM((len(DILATIONS), seq, LANES), F32)],
        compiler_params=_params(("parallel", "parallel")),
    )(qkv, qkv, qkv)


def _mix0_kernel(a_ref, pc_ref, pp_ref, pn_ref, x_ref, wout_ref, pw_ref, ps_ref,
                 out_ref, ext_ref, *, seq, tm):
    ti = pl.program_id(0) % (seq // tm)
    acc = jnp.dot(a_ref[...], wout_ref[:ATTN_WIDTH, :], preferred_element_type=F32)

    ext_ref[:POOL_HALO, :] = jnp.where(ti > 0, pp_ref[...], 0.0)
    ext_ref[POOL_HALO:POOL_HALO + tm, :] = pc_ref[...]
    ext_ref[POOL_HALO + tm:, :] = jnp.where(ti < seq // tm - 1, pn_ref[...], 0.0)
    pos = ti * tm + lax.broadcasted_iota(jnp.int32, (tm, 1), 0)
    for g, win in enumerate(POOL_WINDOWS):
        cols = slice(g * LANES, (g + 1) * LANES)
        before, after = win // 2, win - win // 2
        tot = ext_ref[POOL_HALO - before:POOL_HALO - before + tm, cols]
        for d in range(-before + 1, after):
            tot = tot + ext_ref[POOL_HALO + d:POOL_HALO + d + tm, cols]
        cnt = (jnp.minimum(pos + after, seq) - jnp.maximum(pos - before, 0)).astype(F32)
        diff = tot / cnt - pc_ref[:, cols]
        mixed = jnp.dot(diff.astype(BF16), pw_ref[g], preferred_element_type=F32) * ps_ref[:, cols]
        acc = acc + jnp.dot(mixed.astype(BF16),
                            wout_ref[ATTN_WIDTH + g * LANES:ATTN_WIDTH + (g + 1) * LANES, :],
                            preferred_element_type=F32)
    out_ref[...] = x_ref[...] + acc


def _mix0(a_out, p_in, x2, w_out, pool_w, pool_scale, seq, tm):
    T, D = x2.shape
    pw = p_in.shape[1]
    hb = tm // POOL_HALO
    last = T // POOL_HALO - 1
    row = lambda w: pl.BlockSpec((tm, w), lambda i: (i, 0))
    full = lambda s: pl.BlockSpec(s, lambda i: (0,) * len(s))
    return pl.pallas_call(
        functools.partial(_mix0_kernel, seq=seq, tm=tm),
        grid=(T // tm,),
        in_specs=[
            row(ATTN_WIDTH), row(pw),
            pl.BlockSpec((POOL_HALO, pw), lambda i: (jnp.maximum(i * hb - 1, 0), 0)),
            pl.BlockSpec((POOL_HALO, pw), lambda i: (jnp.minimum((i + 1) * hb, last), 0)),
            row(D), full(w_out.shape), full(pool_w.shape), full((1, pw))],
        out_specs=row(D),
        out_shape=jax.ShapeDtypeStruct((T, D), F32),
        scratch_shapes=[pltpu.VMEM((tm + 2 * POOL_HALO, pw), F32)],
        compiler_params=_params(("parallel",)),
    )(a_out, p_in, p_in, p_in, x2, w_out.astype(BF16), pool_w.astype(BF16),
      pool_scale.reshape(1, pw))


def _sgu_kernel(x_ref, g_ref, win_ref, cn_ref, ws_ref, bs_ref, wout_ref, out_ref, gated_ref, *, tm):
    x = x_ref[...]
    width = wout_ref.shape[0]
    h = _rms(x, g_ref[...]).astype(BF16)
    u = _gelu(jnp.dot(h, win_ref[:, :width], preferred_element_type=F32))
    v = _gelu(jnp.dot(h, win_ref[:, width:], preferred_element_type=F32))
    vb = _rms(v, cn_ref[...]).astype(BF16)
    for n in range(tm // SGU_CHUNK):
        rows = slice(n * SGU_CHUNK, (n + 1) * SGU_CHUNK)
        for g in range(SGU_GROUPS):
            cols = slice(g * LANES, (g + 1) * LANES)
            mixed = jnp.dot(ws_ref[g], vb[rows, cols], preferred_element_type=F32) + bs_ref[:, cols]
            gated_ref[rows, cols] = (u[rows, cols] * mixed).astype(BF16)
    out_ref[...] = x + jnp.dot(gated_ref[...], wout_ref[...], preferred_element_type=F32)


def _sgu(x2, g, w_in, c_norm, w_s, b_s, w_out, tm):
    T, D = x2.shape
    width = w_out.shape[0]
    bias = jnp.repeat(b_s.T, width // SGU_GROUPS, axis=1)
    full = lambda s: pl.BlockSpec(s, lambda i: (0,) * len(s))
    return pl.pallas_call(
        functools.partial(_sgu_kernel, tm=tm),
        grid=(T // tm,),
        in_specs=[pl.BlockSpec((tm, D), lambda i: (i, 0)), full((1, D)), full(w_in.shape),
                  full((1, width)), full(w_s.shape), full(bias.shape), full(w_out.shape)],
        out_specs=pl.BlockSpec((tm, D), lambda i: (i, 0)),
        out_shape=jax.ShapeDtypeStruct((T, D), F32),
        scratch_shapes=[pltpu.VMEM((tm, width), BF16)],
        compiler_params=_params(("parallel",)),
    )(x2, g.reshape(1, D), w_in.astype(BF16), c_norm.reshape(1, width), w_s.astype(BF16), bias,
      w_out.astype(BF16))


def _cmpx(v, i, l, descending):
    hi, lo = jnp.maximum(v[i], v[l]), jnp.minimum(v[i], v[l])
    v[i], v[l] = (hi, lo) if descending else (lo, hi)


def _bitonic_sort_desc(v):
    v = list(v)
    n = len(v)
    k = 2
    while k <= n:
        j = k // 2
        while j >= 1:
            for i in range(n):
                l = i ^ j
                if l > i:
                    _cmpx(v, i, l, (i & k) == 0)
            j //= 2
        k *= 2
    return v


def _merge_top(a, b):
    n = len(a)
    v = [jnp.maximum(a[i], b[n - 1 - i]) for i in range(n)]
    j = n // 2
    while j >= 1:
        for i in range(n):
            if (i & j) == 0:
                _cmpx(v, i, i + j, True)
        j //= 2
    return v


def _top_sorted(vals, k):
    groups = [_bitonic_sort_desc(vals[i:i + k]) for i in range(0, len(vals), k)]
    while len(groups) > 1:
        groups = [_merge_top(groups[i], groups[i + 1]) for i in range(0, len(groups), 2)]
    return groups[0]


def _count_reached(s, t):
    assert len(t) == 16
    c16 = s >= t[15]
    c8 = s >= t[7]
    c4 = s >= jnp.where(c8, t[11], t[3])
    c2 = s >= jnp.where(c8, jnp.where(c4, t[13], t[9]), jnp.where(c4, t[5], t[1]))
    quarter = [jnp.where(c2, t[4 * q + 2], t[4 * q]) for q in range(4)]
    c1 = s >= jnp.where(c8, jnp.where(c4, quarter[3], quarter[2]), jnp.where(c4, quarter[1], quarter[0]))
    return (jnp.where(c8, 8.0, 0.0) + jnp.where(c4, 4.0, 0.0) + jnp.where(c2, 2.0, 0.0)
            + jnp.where(c1, 1.0, 0.0) + jnp.where(c16, 1.0, 0.0))


def _count(preds):
    tot = jnp.where(preds[0], 1.0, 0.0)
    for p in preds[1:]:
        tot = tot + jnp.where(p, 1.0, 0.0)
    return tot


def _twin_bf16(v):
    bits = pltpu.bitcast(v.astype(BF16).astype(F32), U32)
    return bits | (bits >> 16)


def _peer_gate_kernel(x_ref, g_ref, wq_ref, keys_ref, ht_ref, r2_ref, e2_ref, ne_ref,
                      km_ref, om_ref, *, tg):
    K = PEER_TOPK
    nk = PEER_N_KEYS
    chunks = tg // LANES

    @pl.when(pl.program_id(1) == 0)
    def _():
        ht_ref[...] = _as_words(_rms(x_ref[...], g_ref[...]).T)

    qt = jnp.dot(wq_ref[...], _as_bf16(ht_ref[...]), preferred_element_type=F32)
    half_dim = qt.shape[0] // 2
    for side in range(2):
        sc = jnp.dot(keys_ref[side], qt[side * half_dim:(side + 1) * half_dim].astype(BF16),
                     preferred_element_type=F32)
        for c in range(chunks):
            km_ref[side, pl.ds(c, nk, stride=chunks), :] = sc[:, c * LANES:(c + 1) * LANES]

    def keyrows(k):
        return slice(k * chunks, (k + 1) * chunks)

    s1 = [km_ref[0, keyrows(k), :] for k in range(nk)]
    s2 = [km_ref[1, keyrows(k), :] for k in range(nk)]
    A = _top_sorted(s1, K)
    B = _top_sorted(s2, K)

    reach = [K // (i + 1) for i in range(K)]
    cand = [[A[i] + B[j] for j in range(reach[i])] for i in range(K)]
    ninf = jnp.full_like(A[0], -jnp.inf)
    rest = [c for row in cand[1:] for c in row]
    rest = rest + [ninf] * (-len(rest) % K)
    top = cand[0]
    for i in range(0, len(rest), K):
        top = _merge_top(top, _bitonic_sort_desc(rest[i:i + K]))
    thr = top[K - 1]

    gt = [_count([c > thr for c in row]) for row in cand]
    eq = [_count([c == thr for c in row]) for row in cand]
    total_gt = functools.reduce(lambda a, b: a + b, gt)
    room = float(K) - total_gt
    n = []
    for i in range(K):
        n.append(gt[i] + jnp.clip(room, 0.0, eq[i]))
        room = room - eq[i]

    e1 = [jnp.exp(a - A[0]) for a in A]
    e2 = [jnp.exp(b - B[0]) for b in B]
    z = jnp.zeros_like(A[0])
    for i in range(K):
        inner = jnp.zeros_like(z)
        for j in range(reach[i]):
            inner = inner + jnp.where(n[i] > float(j), e2[j], 0.0)
        z = z + e1[i] * inner
    inv_z = 1.0 / z

    pinf = jnp.full_like(A[0], jnp.inf)
    t = []
    for m in range(1, K + 1):
        tm_ = pinf
        for i in range(K):
            tm_ = jnp.minimum(tm_, jnp.where(n[i] >= float(m), A[i], pinf))
        t.append(tm_)

    b_up = B[::-1]
    for k in range(nk):
        om_ref[0, keyrows(k), :] = _count_reached(s1[k], t)
        om_ref[1, keyrows(k), :] = jnp.exp(s1[k] - A[0]) * inv_z
        om_ref[2, keyrows(k), :] = float(K) - _count_reached(s2[k], b_up)
        om_ref[3, keyrows(k), :] = jnp.exp(s2[k] - B[0])

    for c in range(chunks):
        cols = slice(c * LANES, (c + 1) * LANES)
        ne_ref[0, c] = _twin_bf16(om_ref[0, pl.ds(c, nk, stride=chunks), :])
        ne_ref[1, c] = _twin_bf16(om_ref[1, pl.ds(c, nk, stride=chunks), :])
        r2_ref[:, cols] = _as_words(om_ref[2, pl.ds(c, nk, stride=chunks), :])
        e2_ref[:, cols] = _as_words(om_ref[3, pl.ds(c, nk, stride=chunks), :])


def _peer_gates(x2, g, w_q, sub_keys):
    T, D = x2.shape
    tg = VREG_TOKENS
    heads, _, nk, half_dim = sub_keys.shape
    wq_t = w_q.T.astype(BF16)
    return pl.pallas_call(
        functools.partial(_peer_gate_kernel, tg=tg),
        grid=(T // tg, heads),
        in_specs=[pl.BlockSpec((tg, D), lambda i, h: (i, 0)),
                  pl.BlockSpec((1, D), lambda i, h: (0, 0)),
                  pl.BlockSpec((2 * half_dim, D), lambda i, h: (h, 0)),
                  pl.BlockSpec((None, 2, nk, half_dim), lambda i, h: (h, 0, 0, 0))],
        out_specs=[pl.BlockSpec((D // 2, tg), lambda i, h: (0, i)),
                   pl.BlockSpec((nk // 2, tg), lambda i, h: (h, i)),
                   pl.BlockSpec((nk // 2, tg), lambda i, h: (h, i)),
                   pl.BlockSpec((None, 2, tg // LANES, nk, LANES), lambda i, h: (h, 0, i, 0, 0))],
        out_shape=[jax.ShapeDtypeStruct((D // 2, T), U32),
                   jax.ShapeDtypeStruct((heads * nk // 2, T), U32),
                   jax.ShapeDtypeStruct((heads * nk // 2, T), U32),
                   jax.ShapeDtypeStruct((heads, 2, T // LANES, nk, LANES), U32)],
        scratch_shapes=[pltpu.VMEM((2, nk * SUBLANES, LANES), F32),
                        pltpu.VMEM((4, nk * SUBLANES, LANES), F32)],
        compiler_params=_params(("parallel", "arbitrary")),
    )(x2, g.reshape(1, D), wq_t, sub_keys.astype(BF16))


KEY_BLOCK = 4
PIECE_ROWS = 256


def _gate_hidden(hid_ref, ga_ref, r2_ref, e2_ref, ne_ref, first_key, c, al0):
    nk = PEER_N_KEYS
    tiles = range(0, nk, BF16_ROWS)
    cols = slice(c * LANES, (c + 1) * LANES)

    def row_tile(h, q, al):
        return _as_bf16(ne_ref[h, q, c, pl.ds(first_key + al, SUBLANES, stride=0), :])

    gates = [[None for _ in tiles] for _ in range(KEY_BLOCK)]
    for h in range(PEER_HEADS):
        counts = [row_tile(h, 0, al0 + k) for k in range(KEY_BLOCK)]
        factors = [row_tile(h, 1, al0 + k) for k in range(KEY_BLOCK)]
        for ti, r0 in enumerate(tiles):
            hrows = slice((h * nk + r0) // 2, (h * nk + r0) // 2 + SUBLANES)
            rank, e2 = _as_bf16(r2_ref[hrows, cols]), _as_bf16(e2_ref[hrows, cols])
            for k in range(KEY_BLOCK):
                term = jnp.where(rank < counts[k], e2, jnp.zeros_like(e2)) * factors[k]
                gates[k][ti] = term if gates[k][ti] is None else gates[k][ti] + term
    for k in range(KEY_BLOCK):
        for ti, r0 in enumerate(tiles):
            rows = slice((al0 + k) * nk + r0, (al0 + k) * nk + r0 + BF16_ROWS)
            ga_ref[rows, cols] = gates[k][ti] * _gelu(hid_ref[rows, cols].astype(BF16))


def _peer_expert_kernel(ht_ref, wd_ref, wu_ref, r2_ref, e2_ref, ne_ref, x_ref, gf_ref, out_ref,
                        acc_ref, hid0_ref, hid1_ref, ga0_ref, ga1_ref, *, final_norm, n_blocks, n_work):
    s = pl.program_id(0)
    eb = hid0_ref.shape[0]
    gate_item = jnp.clip(s - 1, 0, n_work - 1)
    up_block = jnp.clip(s - 2, 0, n_work - 1) % n_blocks
    first_key = (gate_item % n_blocks) * (eb // PEER_N_KEYS)

    @pl.when(s == 0)
    def _():
        hid1_ref[...] = jnp.zeros_like(hid1_ref)
        ga0_ref[...] = jnp.zeros_like(ga0_ref)

    @pl.when(up_block == 0)
    def _():
        acc_ref[...] = jnp.zeros_like(acc_ref)

    def stages(hid_new, hid_old, ga_new, ga_old):
        tm = hid_new.shape[1]
        d_model = acc_ref.shape[0]

        def up(n, m):
            cols = slice(n * MXU_WIDTH, (n + 1) * MXU_WIDTH)
            rows = slice(m * PIECE_ROWS, (m + 1) * PIECE_ROWS)
            words = slice(m * PIECE_ROWS // 2, (m + 1) * PIECE_ROWS // 2)
            acc_ref[rows, cols] += jnp.dot(_as_bf16(wu_ref[words, :]), ga_old[:, cols],
                                           preferred_element_type=F32)

        def down(n, m):
            cols = slice(n * MXU_WIDTH, (n + 1) * MXU_WIDTH)
            rows = slice(m * PIECE_ROWS, (m + 1) * PIECE_ROWS)
            words = slice(m * PIECE_ROWS // 2, (m + 1) * PIECE_ROWS // 2)
            hid_new[rows, cols] = jnp.dot(_as_bf16(wd_ref[words, :]), _as_bf16(ht_ref[:, cols]),
                                          preferred_element_type=F32)

        ups, downs = d_model // PIECE_ROWS, eb // PIECE_ROWS
        assert downs % ups == 0
        mxu = [functools.partial(f, n, m) for n in range(tm // MXU_WIDTH) for u in range(ups)
               for f, m in [(up, u)] + [(down, u * (downs // ups) + d) for d in range(downs // ups)]]
        vpu = [functools.partial(_gate_hidden, hid_old, ga_new, r2_ref, e2_ref, ne_ref, first_key, c, al0)
               for c in range(tm // LANES) for al0 in range(0, eb // PEER_N_KEYS, KEY_BLOCK)]
        done = 0
        for i, piece in enumerate(mxu):
            piece()
            upto = (i + 1) * len(vpu) // len(mxu)
            for g in vpu[done:upto]:
                g()
            done = upto

    pl.when(s % 2 == 0)(lambda: stages(hid0_ref, hid1_ref, ga1_ref, ga0_ref))
    pl.when(s % 2 == 1)(lambda: stages(hid1_ref, hid0_ref, ga0_ref, ga1_ref))

    @pl.when((s >= 2) & (up_block == n_blocks - 1))
    def _():
        y = x_ref[...] + acc_ref[...].T
        out_ref[...] = _rms(y, gf_ref[...]) if final_norm else y


def _peer_experts(ht, r2, e2, ne, x2, wd_words, wu_words, layer, g_final, final_norm, tm, eb):
    T, D = x2.shape
    n_exp = wu_words.shape[2]
    heads, _, _, nk, _ = ne.shape
    n_blocks = n_exp // eb
    n_work = (T // tm) * n_blocks

    def item(lag):
        def split(s):
            w = jnp.clip(s - lag, 0, n_work - 1)
            return w // n_blocks, w % n_blocks
        return split

    down, gate, up = item(0), item(1), item(2)
    return pl.pallas_call(
        functools.partial(_peer_expert_kernel, final_norm=final_norm, n_blocks=n_blocks, n_work=n_work),
        grid=(n_work + 2,),
        in_specs=[pl.BlockSpec((D // 2, tm), lambda s: (0, down(s)[0])),
                  pl.BlockSpec((None, eb // 2, D), lambda s: (layer, down(s)[1], 0)),
                  pl.BlockSpec((None, D // 2, eb), lambda s: (layer, 0, up(s)[1])),
                  pl.BlockSpec((heads * nk // 2, tm), lambda s: (0, gate(s)[0])),
                  pl.BlockSpec((heads * nk // 2, tm), lambda s: (0, gate(s)[0])),
                  pl.BlockSpec((heads, 2, tm // LANES, nk, LANES), lambda s: (0, 0, gate(s)[0], 0, 0)),
                  pl.BlockSpec((tm, D), lambda s: (up(s)[0], 0)),
                  pl.BlockSpec((1, D), lambda s: (0, 0))],
        out_specs=pl.BlockSpec((tm, D), lambda s: (up(s)[0], 0)),
        out_shape=jax.ShapeDtypeStruct((T, D), F32),
        scratch_shapes=[pltpu.VMEM((D, tm), F32), pltpu.VMEM((eb, tm), F32), pltpu.VMEM((eb, tm), F32),
                        pltpu.VMEM((eb, tm), BF16), pltpu.VMEM((eb, tm), BF16)],
        compiler_params=_params(("arbitrary",)),
    )(ht, wd_words, wu_words, r2, e2, ne, x2, g_final.reshape(1, D))


def kernel(x, norm_mix, norm_ffn, norm_final, ab_w_in, ab_w_out, pool_w, pool_scale, c_w_in,
           c_norm, c_w_s, c_b_s, c_w_out, peer_w_q, peer_sub_keys, peer_w_down, peer_w_up):
    B, S, D = x.shape
    depth = norm_mix.shape[0]
    tm = 512
    eb = 2048
    assert S % tm == 0 and (B * S) % VREG_TOKENS == 0
    assert all(S % (d * QUERY_BLOCK) == 0 for d in DILATIONS)
    x2 = x.reshape(B * S, D)
    wd_words = _pack_rows(peer_w_down, transpose=False)
    wu_words = _pack_rows(peer_w_up, transpose=True)
    for layer in range(depth):
        j = layer // 2
        if layer % 2 == 0:
            qkv, p_in = _inproj(x2, norm_mix[layer], ab_w_in[j], S, tm)
            a_out = _attention(qkv, B, S)
            x2 = _mix0(a_out, p_in, x2, ab_w_out[j], pool_w[j], pool_scale[j], S, tm)
        else:
            x2 = _sgu(x2, norm_mix[layer], c_w_in[j], c_norm[j], c_w_s[j], c_b_s[j], c_w_out[j], tm)
        ht, r2, e2, ne = _peer_gates(x2, norm_ffn[layer], peer_w_q[layer], peer_sub_keys[layer])
        x2 = _peer_experts(ht, r2, e2, ne, x2, wd_words, wu_words, layer, norm_final,
                           layer == depth - 1, tm, eb)
    return x2.reshape(B, S, D)
```

```python
import functools
import math

import jax
import jax.numpy as jnp
import numpy as np
from jax import lax
from jax.experimental import pallas as pl
from jax.experimental.pallas import tpu as pltpu

F32 = jnp.float32
BF16 = jnp.bfloat16

LANES = 128
SUBLANES = 8
BF16_ROWS = 2 * SUBLANES
MXU_WIDTH = 256
VREG_TOKENS = SUBLANES * LANES
VMEM_LIMIT = 56 * 1024 * 1024

ATTN_HEADS = 8
HEAD_DIM = 64
ATTN_WIDTH = ATTN_HEADS * HEAD_DIM
DILATIONS = (1, 4, 16)
ATTN_HALF = 64
QUERY_BLOCK = 128
BLOCK_UNROLL = 8
ROPE_THETA = 500000.0
ROPE_DIM = HEAD_DIM // 4
POOL_WINDOWS = (2, 4, 8, 16)
POOL_HALO = 8
SGU_CHUNK = 128
SGU_GROUPS = 8
PEER_HEADS = 8
PEER_N_KEYS = 128
PEER_TOPK = 16
RMS_EPS = 1e-6
NEG_INF = -1e30
GELU_C = math.sqrt(2.0 / math.pi)


def _params(semantics):
    return pltpu.CompilerParams(dimension_semantics=semantics, vmem_limit_bytes=VMEM_LIMIT)


def _rms(x, g):
    return x * lax.rsqrt(jnp.mean(x * x, axis=-1, keepdims=True) + RMS_EPS) * g


U32 = jnp.uint32


def _as_bf16(words):
    return pltpu.bitcast(words, BF16)


def _as_words(x):
    return pltpu.bitcast(x.astype(BF16), U32)


def _pack_kernel(w_ref, o_ref, *, transpose):
    w = w_ref[...]
    o_ref[...] = _as_words(w.T if transpose else w)


def _pack_rows(w, transpose, rows=512):
    N, R, C = w.shape
    out_block, out_map = ((None, C // 2, rows), lambda n, i: (n, 0, i)) if transpose else (
        (None, rows // 2, C), lambda n, i: (n, i, 0))
    out_shape = (N, C // 2, R) if transpose else (N, R // 2, C)
    return pl.pallas_call(
        functools.partial(_pack_kernel, transpose=transpose),
        grid=(N, R // rows),
        in_specs=[pl.BlockSpec((None, rows, C), lambda n, i: (n, i, 0))],
        out_specs=pl.BlockSpec(out_block, out_map),
        out_shape=jax.ShapeDtypeStruct(out_shape, U32),
        compiler_params=_params(("parallel", "parallel")),
    )(w)


def _gelu(x):
    half = 0.5 * x
    return half + half * jnp.tanh(x * (GELU_C + (GELU_C * 0.044715) * (x * x)))


def _inproj_kernel(x_ref, g_ref, w_ref, cos_ref, sa_ref, sb_ref, qkv_ref, p_ref):
    h = _rms(x_ref[...], g_ref[...]).astype(BF16)
    proj = jnp.dot(h, w_ref[...], preferred_element_type=F32)
    c, sa, sb = cos_ref[...], sa_ref[...], sb_ref[...]
    half = ROPE_DIM // 2
    for slab in range(2 * ATTN_WIDTH // LANES):
        t = proj[:, slab * LANES:(slab + 1) * LANES]
        up = pltpu.roll(t, LANES - half, axis=1)
        dn = pltpu.roll(t, half, axis=1)
        qkv_ref[slab] = t * c + up * sa + dn * sb
    for slab in range(2 * ATTN_WIDTH // LANES, 3 * ATTN_WIDTH // LANES):
        qkv_ref[slab] = proj[:, slab * LANES:(slab + 1) * LANES]
    p_ref[...] = proj[:, 3 * ATTN_WIDTH:]


def _rope_tables(seq):
    half = ROPE_DIM // 2
    pos = jnp.arange(seq, dtype=F32)
    inv_freq = ROPE_THETA ** (-jnp.arange(0, ROPE_DIM, 2, dtype=F32) / ROPE_DIM)
    ang = pos[:, None] * inv_freq[None, :]
    cos, sin = jnp.cos(ang), jnp.sin(ang)
    pad = HEAD_DIM - ROPE_DIM
    ones = jnp.ones((seq, pad), F32)
    zeros = jnp.zeros((seq, pad), F32)
    zh = jnp.zeros((seq, half), F32)
    c = jnp.concatenate([cos, cos, ones], axis=1)
    sa = jnp.concatenate([-sin, zh, zeros], axis=1)
    sb = jnp.concatenate([zh, sin, zeros], axis=1)
    return tuple(jnp.concatenate([t, t], axis=1) for t in (c, sa, sb))


def _inproj(x2, g, w_in, seq, tm):
    T, D = x2.shape
    n_out = w_in.shape[1]
    pool_w = n_out - 3 * ATTN_WIDTH
    c, sa, sb = _rope_tables(seq)
    spt = seq // tm
    tab = pl.BlockSpec((tm, LANES), lambda i: (i % spt, 0))
    return pl.pallas_call(
        _inproj_kernel,
        grid=(T // tm,),
        in_specs=[pl.BlockSpec((tm, D), lambda i: (i, 0)),
                  pl.BlockSpec((1, D), lambda i: (0, 0)),
                  pl.BlockSpec((D, n_out), lambda i: (0, 0)),
                  tab, tab, tab],
        out_specs=[pl.BlockSpec((3 * ATTN_WIDTH // LANES, tm, LANES), lambda i: (0, i, 0)),
                   pl.BlockSpec((tm, pool_w), lambda i: (i, 0))],
        out_shape=[jax.ShapeDtypeStruct((3 * ATTN_WIDTH // LANES, T, LANES), F32),
                   jax.ShapeDtypeStruct((T, pool_w), F32)],
        compiler_params=_params(("parallel",)),
    )(x2, g.reshape(1, D), w_in.astype(BF16), c, sa, sb)


def _attn_kernel(q_ref, k_ref, v_ref, a_ref, o_scr, lse_scr, *, seq):
    scale = HEAD_DIM ** -0.5
    lane = lax.broadcasted_iota(jnp.int32, (QUERY_BLOCK, LANES), 1)
    first_head = lane < HEAD_DIM
    head_masks = (first_head, jnp.logical_not(first_head))

    for pat, dil in enumerate(DILATIONS):
        length = seq // dil
        kw = min(QUERY_BLOCK + 2 * ATTN_HALF, length)
        blocks = length // QUERY_BLOCK
        ones = jnp.ones((kw, LANES), BF16)

        def group(g, carry, dil=dil, length=length, kw=kw, blocks=blocks, pat=pat, ones=ones):
            loaded = []
            for j in range(BLOCK_UNROLL):
                it = g * BLOCK_UNROLL + j
                res, qb = it // blocks, it % blocks
                q0 = qb * QUERY_BLOCK
                start = jnp.clip(q0 - ATTN_HALF, 0, length - kw)

                def rows(ref, first, count, res=res):
                    return ref[pl.ds(res + dil * first, count, stride=dil), :]

                qpos = q0 + lax.broadcasted_iota(jnp.int32, (QUERY_BLOCK, kw), 0)
                kpos = start + lax.broadcasted_iota(jnp.int32, (QUERY_BLOCK, kw), 1)
                loaded.append((rows(q_ref, q0, QUERY_BLOCK).astype(BF16), rows(k_ref, start, kw).astype(BF16),
                               rows(v_ref, start, kw).astype(BF16), jnp.abs(kpos - qpos) <= ATTN_HALF,
                               pl.ds(res + dil * q0, QUERY_BLOCK, stride=dil)))
            chains = [(j, mask) for j in range(BLOCK_UNROLL) for mask in head_masks]
            scores = []
            for j, mask in chains:
                qp, kp, _, valid, _ = loaded[j]
                qm = jnp.where(mask, qp, jnp.zeros_like(qp))
                s = lax.dot_general(qm, kp, (((1,), (1,)), ((), ())), preferred_element_type=F32) * scale
                scores.append(jnp.where(valid, s, NEG_INF))
            tops = [jnp.max(s, axis=-1, keepdims=True) for s in scores]
            probs = [jnp.exp(s - m).astype(BF16) for s, m in zip(scores, tops)]
            totals = [jnp.dot(p, ones, preferred_element_type=F32) for p in probs]
            mixes = [jnp.dot(p, loaded[j][2], preferred_element_type=F32) for p, (j, _) in zip(probs, chains)]
            outs = [o / t for o, t in zip(mixes, totals)]
            lses = [m + jnp.log(t) for m, t in zip(tops, totals)]
            for j in range(BLOCK_UNROLL):
                dst = loaded[j][4]
                o_scr[pat, dst, :] = jnp.where(first_head, outs[2 * j], outs[2 * j + 1])
                lse_scr[pat, dst, :] = jnp.where(first_head, lses[2 * j], lses[2 * j + 1])
            return carry

        assert (dil * blocks) % BLOCK_UNROLL == 0
        lax.fori_loop(0, dil * blocks // BLOCK_UNROLL, group, 0)

    la, lb, lc = lse_scr[0], lse_scr[1], lse_scr[2]
    m = jnp.maximum(jnp.maximum(la, lb), lc)
    ea, eb, ec = jnp.exp(la - m), jnp.exp(lb - m), jnp.exp(lc - m)
    a_ref[...] = ((o_scr[0] * ea + o_scr[1] * eb + o_scr[2] * ec) / (ea + eb + ec)).astype(BF16)


def _attention(qkv, batch, seq):
    pairs = ATTN_WIDTH // LANES

    def part(j):
        return pl.BlockSpec((None, seq, LANES), lambda b, hp: (j * pairs + hp, b, 0))

    return pl.pallas_call(
        functools.partial(_attn_kernel, seq=seq),
        grid=(batch, pairs),
        in_specs=[part(0), part(1), part(2)],
        out_specs=pl.BlockSpec((seq, LANES), lambda b, hp: (b, hp)),
        out_shape=jax.ShapeDtypeStruct((batch * seq, ATTN_WIDTH), BF16),
        scratch_shapes=[pltpu.VMEM((len(DILATIONS), seq, LANES), F32),
                        pltpu.VMEM((len(DILATIONS), seq, LANES), F32)],
        compiler_params=_params(("parallel", "parallel")),
    )(qkv, qkv, qkv)


def _mix0_kernel(a_ref, pc_ref, pp_ref, pn_ref, x_ref, wout_ref, pw_ref, ps_ref,
                 out_ref, ext_ref, *, seq, tm):
    ti = pl.program_id(0) % (seq // tm)
    acc = jnp.dot(a_ref[...], wout_ref[:ATTN_WIDTH, :], preferred_element_type=F32)

    ext_ref[:POOL_HALO, :] = jnp.where(ti > 0, pp_ref[...], 0.0)
    ext_ref[POOL_HALO:POOL_HALO + tm, :] = pc_ref[...]
    ext_ref[POOL_HALO + tm:, :] = jnp.where(ti < seq // tm - 1, pn_ref[...], 0.0)
    pos = ti * tm + lax.broadcasted_iota(jnp.int32, (tm, 1), 0)
    for g, win in enumerate(POOL_WINDOWS):
        cols = slice(g * LANES, (g + 1) * LANES)
        before, after = win // 2, win - win // 2
        tot = ext_ref[POOL_HALO - before:POOL_HALO - before + tm, cols]
        for d in range(-before + 1, after):
            tot = tot + ext_ref[POOL_HALO + d:POOL_HALO + d + tm, cols]
        cnt = (jnp.minimum(pos + after, seq) - jnp.maximum(pos - before, 0)).astype(F32)
        diff = tot / cnt - pc_ref[:, cols]
        mixed = jnp.dot(diff.astype(BF16), pw_ref[g], preferred_element_type=F32) * ps_ref[:, cols]
        acc = acc + jnp.dot(mixed.astype(BF16),
                            wout_ref[ATTN_WIDTH + g * LANES:ATTN_WIDTH + (g + 1) * LANES, :],
                            preferred_element_type=F32)
    out_ref[...] = x_ref[...] + acc


def _mix0(a_out, p_in, x2, w_out, pool_w, pool_scale, seq, tm):
    T, D = x2.shape
    pw = p_in.shape[1]
    hb = tm // POOL_HALO
    last = T // POOL_HALO - 1
    row = lambda w: pl.BlockSpec((tm, w), lambda i: (i, 0))
    full = lambda s: pl.BlockSpec(s, lambda i: (0,) * len(s))
    return pl.pallas_call(
        functools.partial(_mix0_kernel, seq=seq, tm=tm),
        grid=(T // tm,),
        in_specs=[
            row(ATTN_WIDTH), row(pw),
            pl.BlockSpec((POOL_HALO, pw), lambda i: (jnp.maximum(i * hb - 1, 0), 0)),
            pl.BlockSpec((POOL_HALO, pw), lambda i: (jnp.minimum((i + 1) * hb, last), 0)),
            row(D), full(w_out.shape), full(pool_w.shape), full((1, pw))],
        out_specs=row(D),
        out_shape=jax.ShapeDtypeStruct((T, D), F32),
        scratch_shapes=[pltpu.VMEM((tm + 2 * POOL_HALO, pw), F32)],
        compiler_params=_params(("parallel",)),
    )(a_out, p_in, p_in, p_in, x2, w_out.astype(BF16), pool_w.astype(BF16),
      pool_scale.reshape(1, pw))


def _sgu_kernel(x_ref, g_ref, win_ref, cn_ref, ws_ref, bs_ref, wout_ref, out_ref, gated_ref, *, tm):
    x = x_ref[...]
    width = wout_ref.shape[0]
    h = _rms(x, g_ref[...]).astype(BF16)
    u = _gelu(jnp.dot(h, win_ref[:, :width], preferred_element_type=F32))
    v = _gelu(jnp.dot(h, win_ref[:, width:], preferred_element_type=F32))
    vb = _rms(v, cn_ref[...]).astype(BF16)
    for n in range(tm // SGU_CHUNK):
        rows = slice(n * SGU_CHUNK, (n + 1) * SGU_CHUNK)
        for g in range(SGU_GROUPS):
            cols = slice(g * LANES, (g + 1) * LANES)
            mixed = jnp.dot(ws_ref[g], vb[rows, cols], preferred_element_type=F32) + bs_ref[:, cols]
            gated_ref[rows, cols] = (u[rows, cols] * mixed).astype(BF16)
    out_ref[...] = x + jnp.dot(gated_ref[...], wout_ref[...], preferred_element_type=F32)


def _sgu(x2, g, w_in, c_norm, w_s, b_s, w_out, tm):
    T, D = x2.shape
    width = w_out.shape[0]
    bias = jnp.repeat(b_s.T, width // SGU_GROUPS, axis=1)
    full = lambda s: pl.BlockSpec(s, lambda i: (0,) * len(s))
    return pl.pallas_call(
        functools.partial(_sgu_kernel, tm=tm),
        grid=(T // tm,),
        in_specs=[pl.BlockSpec((tm, D), lambda i: (i, 0)), full((1, D)), full(w_in.shape),
                  full((1, width)), full(w_s.shape), full(bias.shape), full(w_out.shape)],
        out_specs=pl.BlockSpec((tm, D), lambda i: (i, 0)),
        out_shape=jax.ShapeDtypeStruct((T, D), F32),
        scratch_shapes=[pltpu.VMEM((tm, width), BF16)],
        compiler_params=_params(("parallel",)),
    )(x2, g.reshape(1, D), w_in.astype(BF16), c_norm.reshape(1, width), w_s.astype(BF16), bias,
      w_out.astype(BF16))


def _cmpx(v, i, l, descending):
    hi, lo = jnp.maximum(v[i], v[l]), jnp.minimum(v[i], v[l])
    v[i], v[l] = (hi, lo) if descending else (lo, hi)


def _bitonic_sort_desc(v):
    v = list(v)
    n = len(v)
    k = 2
    while k <= n:
        j = k // 2
        while j >= 1:
            for i in range(n):
                l = i ^ j
                if l > i:
                    _cmpx(v, i, l, (i & k) == 0)
            j //= 2
        k *= 2
    return v


def _merge_top(a, b):
    n = len(a)
    v = [jnp.maximum(a[i], b[n - 1 - i]) for i in range(n)]
    j = n // 2
    while j >= 1:
        for i in range(n):
            if (i & j) == 0:
                _cmpx(v, i, i + j, True)
        j //= 2
    return v


def _top_sorted(vals, k):
    groups = [_bitonic_sort_desc(vals[i:i + k]) for i in range(0, len(vals), k)]
    while len(groups) > 1:
        groups = [_merge_top(groups[i], groups[i + 1]) for i in range(0, len(groups), 2)]
    return groups[0]


def _count_reached(s, t):
    assert len(t) == 16
    c16 = s >= t[15]
    c8 = s >= t[7]
    c4 = s >= jnp.where(c8, t[11], t[3])
    c2 = s >= jnp.where(c8, jnp.where(c4, t[13], t[9]), jnp.where(c4, t[5], t[1]))
    quarter = [jnp.where(c2, t[4 * q + 2], t[4 * q]) for q in range(4)]
    c1 = s >= jnp.where(c8, jnp.where(c4, quarter[3], quarter[2]), jnp.where(c4, quarter[1], quarter[0]))
    return (jnp.where(c8, 8.0, 0.0) + jnp.where(c4, 4.0, 0.0) + jnp.where(c2, 2.0, 0.0)
            + jnp.where(c1, 1.0, 0.0) + jnp.where(c16, 1.0, 0.0))


def _count(preds):
    tot = jnp.where(preds[0], 1.0, 0.0)
    for p in preds[1:]:
        tot = tot + jnp.where(p, 1.0, 0.0)
    return tot


def _twin_bf16(v):
    bits = pltpu.bitcast(v.astype(BF16).astype(F32), U32)
    return bits | (bits >> 16)


def _peer_gate_kernel(x_ref, g_ref, wq_ref, keys_ref, ht_ref, r2_ref, e2_ref, ne_ref,
                      km_ref, om_ref, *, tg):
    K = PEER_TOPK
    nk = PEER_N_KEYS
    chunks = tg // LANES

    @pl.when(pl.program_id(1) == 0)
    def _():
        ht_ref[...] = _as_words(_rms(x_ref[...], g_ref[...]).T)

    qt = jnp.dot(wq_ref[...], _as_bf16(ht_ref[...]), preferred_element_type=F32)
    half_dim = qt.shape[0] // 2
    for side in range(2):
        sc = jnp.dot(keys_ref[side], qt[side * half_dim:(side + 1) * half_dim].astype(BF16),
                     preferred_element_type=F32)
        for c in range(chunks):
            km_ref[side, pl.ds(c, nk, stride=chunks), :] = sc[:, c * LANES:(c + 1) * LANES]

    def keyrows(k):
        return slice(k * chunks, (k + 1) * chunks)

    s1 = [km_ref[0, keyrows(k), :] for k in range(nk)]
    s2 = [km_ref[1, keyrows(k), :] for k in range(nk)]
    A = _top_sorted(s1, K)
    B = _top_sorted(s2, K)

    reach = [K // (i + 1) for i in range(K)]
    cand = [[A[i] + B[j] for j in range(reach[i])] for i in range(K)]
    ninf = jnp.full_like(A[0], -jnp.inf)
    rest = [c for row in cand[1:] for c in row]
    rest = rest + [ninf] * (-len(rest) % K)
    top = cand[0]
    for i in range(0, len(rest), K):
        top = _merge_top(top, _bitonic_sort_desc(rest[i:i + K]))
    thr = top[K - 1]

    gt = [_count([c > thr for c in row]) for row in cand]
    eq = [_count([c == thr for c in row]) for row in cand]
    total_gt = functools.reduce(lambda a, b: a + b, gt)
    room = float(K) - total_gt
    n = []
    for i in range(K):
        n.append(gt[i] + jnp.clip(room, 0.0, eq[i]))
        room = room - eq[i]

    e1 = [jnp.exp(a - A[0]) for a in A]
    e2 = [jnp.exp(b - B[0]) for b in B]
    z = jnp.zeros_like(A[0])
    for i in range(K):
        inner = jnp.zeros_like(z)
        for j in range(reach[i]):
            inner = inner + jnp.where(n[i] > float(j), e2[j], 0.0)
        z = z + e1[i] * inner
    inv_z = 1.0 / z

    pinf = jnp.full_like(A[0], jnp.inf)
    t = []
    for m in range(1, K + 1):
        tm_ = pinf
        for i in range(K):
            tm_ = jnp.minimum(tm_, jnp.where(n[i] >= float(m), A[i], pinf))
        t.append(tm_)

    b_up = B[::-1]
    for k in range(nk):
        om_ref[0, keyrows(k), :] = _count_reached(s1[k], t)
        om_ref[1, keyrows(k), :] = jnp.exp(s1[k] - A[0]) * inv_z
        om_ref[2, keyrows(k), :] = float(K) - _count_reached(s2[k], b_up)
        om_ref[3, keyrows(k), :] = jnp.exp(s2[k] - B[0])

    for c in range(chunks):
        cols = slice(c * LANES, (c + 1) * LANES)
        ne_ref[0, c] = _twin_bf16(om_ref[0, pl.ds(c, nk, stride=chunks), :])
        ne_ref[1, c] = _twin_bf16(om_ref[1, pl.ds(c, nk, stride=chunks), :])
        r2_ref[:, cols] = _as_words(om_ref[2, pl.ds(c, nk, stride=chunks), :])
        e2_ref[:, cols] = _as_words(om_ref[3, pl.ds(c, nk, stride=chunks), :])


def _peer_gates(x2, g, w_q, sub_keys):
    T, D = x2.shape
    tg = VREG_TOKENS
    heads, _, nk, half_dim = sub_keys.shape
    wq_t = w_q.T.astype(BF16)
    return pl.pallas_call(
        functools.partial(_peer_gate_kernel, tg=tg),
        grid=(T // tg, heads),
        in_specs=[pl.BlockSpec((tg, D), lambda i, h: (i, 0)),
                  pl.BlockSpec((1, D), lambda i, h: (0, 0)),
                  pl.BlockSpec((2 * half_dim, D), lambda i, h: (h, 0)),
                  pl.BlockSpec((None, 2, nk, half_dim), lambda i, h: (h, 0, 0, 0))],
        out_specs=[pl.BlockSpec((D // 2, tg), lambda i, h: (0, i)),
                   pl.BlockSpec((nk // 2, tg), lambda i, h: (h, i)),
                   pl.BlockSpec((nk // 2, tg), lambda i, h: (h, i)),
                   pl.BlockSpec((None, 2, tg // LANES, nk, LANES), lambda i, h: (h, 0, i, 0, 0))],
        out_shape=[jax.ShapeDtypeStruct((D // 2, T), U32),
                   jax.ShapeDtypeStruct((heads * nk // 2, T), U32),
                   jax.ShapeDtypeStruct((heads * nk // 2, T), U32),
                   jax.ShapeDtypeStruct((heads, 2, T // LANES, nk, LANES), U32)],
        scratch_shapes=[pltpu.VMEM((2, nk * SUBLANES, LANES), F32),
                        pltpu.VMEM((4, nk * SUBLANES, LANES), F32)],
        compiler_params=_params(("parallel", "arbitrary")),
    )(x2, g.reshape(1, D), wq_t, sub_keys.astype(BF16))


KEY_BLOCK = 4
PIECE_ROWS = 256


def _gate_hidden(hid_ref, ga_ref, r2_ref, e2_ref, ne_ref, first_key, c, al0):
    nk = PEER_N_KEYS
    tiles = range(0, nk, BF16_ROWS)
    cols = slice(c * LANES, (c + 1) * LANES)

    def row_tile(h, q, al):
        return _as_bf16(ne_ref[h, q, c, pl.ds(first_key + al, SUBLANES, stride=0), :])

    gates = [[None for _ in tiles] for _ in range(KEY_BLOCK)]
    for h in range(PEER_HEADS):
        counts = [row_tile(h, 0, al0 + k) for k in range(KEY_BLOCK)]
        factors = [row_tile(h, 1, al0 + k) for k in range(KEY_BLOCK)]
        for ti, r0 in enumerate(tiles):
            hrows = slice((h * nk + r0) // 2, (h * nk + r0) // 2 + SUBLANES)
            rank, e2 = _as_bf16(r2_ref[hrows, cols]), _as_bf16(e2_ref[hrows, cols])
            for k in range(KEY_BLOCK):
                term = jnp.where(rank < counts[k], e2, jnp.zeros_like(e2)) * factors[k]
                gates[k][ti] = term if gates[k][ti] is None else gates[k][ti] + term
    for k in range(KEY_BLOCK):
        for ti, r0 in enumerate(tiles):
            rows = slice((al0 + k) * nk + r0, (al0 + k) * nk + r0 + BF16_ROWS)
            ga_ref[rows, cols] = gates[k][ti] * _gelu(hid_ref[rows, cols].astype(BF16))


def _peer_expert_kernel(ht_ref, wd_ref, wu_ref, r2_ref, e2_ref, ne_ref, x_ref, gf_ref, out_ref,
                        acc_ref, hid0_ref, hid1_ref, ga0_ref, ga1_ref, *, final_norm, n_blocks, n_work):
    s = pl.program_id(0)
    eb = hid0_ref.shape[0]
    gate_item = jnp.clip(s - 1, 0, n_work - 1)
    up_block = jnp.clip(s - 2, 0, n_work - 1) % n_blocks
    first_key = (gate_item % n_blocks) * (eb // PEER_N_KEYS)

    @pl.when(s == 0)
    def _():
        hid1_ref[...] = jnp.zeros_like(hid1_ref)
        ga0_ref[...] = jnp.zeros_like(ga0_ref)

    @pl.when(up_block == 0)
    def _():
        acc_ref[...] = jnp.zeros_like(acc_ref)

    def stages(hid_new, hid_old, ga_new, ga_old):
        tm = hid_new.shape[1]
        d_model = acc_ref.shape[0]

        def up(n, m):
            cols = slice(n * MXU_WIDTH, (n + 1) * MXU_WIDTH)
            rows = slice(m * PIECE_ROWS, (m + 1) * PIECE_ROWS)
            words = slice(m * PIECE_ROWS // 2, (m + 1) * PIECE_ROWS // 2)
            acc_ref[rows, cols] += jnp.dot(_as_bf16(wu_ref[words, :]), ga_old[:, cols],
                                           preferred_element_type=F32)

        def down(n, m):
            cols = slice(n * MXU_WIDTH, (n + 1) * MXU_WIDTH)
            rows = slice(m * PIECE_ROWS, (m + 1) * PIECE_ROWS)
            words = slice(m * PIECE_ROWS // 2, (m + 1) * PIECE_ROWS // 2)
            hid_new[rows, cols] = jnp.dot(_as_bf16(wd_ref[words, :]), _as_bf16(ht_ref[:, cols]),
                                          preferred_element_type=F32)

        ups, downs = d_model // PIECE_ROWS, eb // PIECE_ROWS
        assert downs % ups == 0
        mxu = [functools.partial(f, n, m) for n in range(tm // MXU_WIDTH) for u in range(ups)
               for f, m in [(up, u)] + [(down, u * (downs // ups) + d) for d in range(downs // ups)]]
        vpu = [functools.partial(_gate_hidden, hid_old, ga_new, r2_ref, e2_ref, ne_ref, first_key, c, al0)
               for al0 in range(0, eb // PEER_N_KEYS, KEY_BLOCK) for c in range(tm // LANES)]
        done = 0
        for i, piece in enumerate(mxu):
            piece()
            upto = -(-(i + 1) * len(vpu) // len(mxu))
            for g in vpu[done:upto]:
                g()
            done = upto

    pl.when(s % 2 == 0)(lambda: stages(hid0_ref, hid1_ref, ga1_ref, ga0_ref))
    pl.when(s % 2 == 1)(lambda: stages(hid1_ref, hid0_ref, ga0_ref, ga1_ref))

    @pl.when((s >= 2) & (up_block == n_blocks - 1))
    def _():
        y = x_ref[...] + acc_ref[...].T
        out_ref[...] = _rms(y, gf_ref[...]) if final_norm else y


def _peer_experts(ht, r2, e2, ne, x2, wd_words, wu_words, layer, g_final, final_norm, tm, eb):
    T, D = x2.shape
    n_exp = wu_words.shape[2]
    heads, _, _, nk, _ = ne.shape
    n_blocks = n_exp // eb
    n_work = (T // tm) * n_blocks

    def item(lag):
        def split(s):
            w = jnp.clip(s - lag, 0, n_work - 1)
            return w // n_blocks, w % n_blocks
        return split

    down, gate, up = item(0), item(1), item(2)
    return pl.pallas_call(
        functools.partial(_peer_expert_kernel, final_norm=final_norm, n_blocks=n_blocks, n_work=n_work),
        grid=(n_work + 2,),
        in_specs=[pl.BlockSpec((D // 2, tm), lambda s: (0, down(s)[0])),
                  pl.BlockSpec((None, eb // 2, D), lambda s: (layer, down(s)[1], 0)),
                  pl.BlockSpec((None, D // 2, eb), lambda s: (layer, 0, up(s)[1])),
                  pl.BlockSpec((heads * nk // 2, tm), lambda s: (0, gate(s)[0])),
                  pl.BlockSpec((heads * nk // 2, tm), lambda s: (0, gate(s)[0])),
                  pl.BlockSpec((heads, 2, tm // LANES, nk, LANES), lambda s: (0, 0, gate(s)[0], 0, 0)),
                  pl.BlockSpec((tm, D), lambda s: (up(s)[0], 0)),
                  pl.BlockSpec((1, D), lambda s: (0, 0))],
        out_specs=pl.BlockSpec((tm, D), lambda s: (up(s)[0], 0)),
        out_shape=jax.ShapeDtypeStruct((T, D), F32),
        scratch_shapes=[pltpu.VMEM((D, tm), F32), pltpu.VMEM((eb, tm), F32), pltpu.VMEM((eb, tm), F32),
                        pltpu.VMEM((eb, tm), BF16), pltpu.VMEM((eb, tm), BF16)],
        compiler_params=_params(("arbitrary",)),
    )(ht, wd_words, wu_words, r2, e2, ne, x2, g_final.reshape(1, D))


def kernel(x, norm_mix, norm_ffn, norm_final, ab_w_in, ab_w_out, pool_w, pool_scale, c_w_in,
           c_norm, c_w_s, c_b_s, c_w_out, peer_w_q, peer_sub_keys, peer_w_down, peer_w_up):
    B, S, D = x.shape
    depth = norm_mix.shape[0]
    tm = 512
    eb = 2048
    assert S % tm == 0 and (B * S) % VREG_TOKENS == 0
    assert all(S % (d * QUERY_BLOCK) == 0 for d in DILATIONS)
    x2 = x.reshape(B * S, D)
    wd_words = _pack_rows(peer_w_down, transpose=False)
    wu_words = _pack_rows(peer_w_up, transpose=True)
    for layer in range(depth):
        j = layer // 2
        if layer % 2 == 0:
            qkv, p_in = _inproj(x2, norm_mix[layer], ab_w_in[j], S, tm)
            a_out = _attention(qkv, B, S)
            x2 = _mix0(a_out, p_in, x2, ab_w_out[j], pool_w[j], pool_scale[j], S, tm)
        else:
            x2 = _sgu(x2, norm_mix[layer], c_w_in[j], c_norm[j], c_w_s[j], c_b_s[j], c_w_out[j], tm)
        ht, r2, e2, ne = _peer_gates(x2, norm_ffn[layer], peer_w_q[layer], peer_sub_keys[layer])
        x2 = _peer_experts(ht, r2, e2, ne, x2, wd_words, wu_words, layer, norm_final,
                           layer == depth - 1, tm, eb)
    return x2.reshape(B, S, D)
```

```python
import functools
import math

import jax
import jax.numpy as jnp
import numpy as np
from jax import lax
from jax.experimental import pallas as pl
from jax.experimental.pallas import tpu as pltpu

F32 = jnp.float32
BF16 = jnp.bfloat16

LANES = 128
SUBLANES = 8
BF16_ROWS = 2 * SUBLANES
MXU_WIDTH = 256
VREG_TOKENS = SUBLANES * LANES
VMEM_LIMIT = 56 * 1024 * 1024

ATTN_HEADS = 8
HEAD_DIM = 64
ATTN_WIDTH = ATTN_HEADS * HEAD_DIM
DILATIONS = (1, 4, 16)
ATTN_HALF = 64
QUERY_BLOCK = 128
BLOCK_UNROLL = 8
ROPE_THETA = 500000.0
ROPE_DIM = HEAD_DIM // 4
POOL_WINDOWS = (2, 4, 8, 16)
POOL_HALO = 8
SGU_CHUNK = 128
SGU_GROUPS = 8
PEER_HEADS = 8
PEER_N_KEYS = 128
PEER_TOPK = 16
RMS_EPS = 1e-6
NEG_INF = -1e30
GELU_C = math.sqrt(2.0 / math.pi)


def _params(semantics):
    return pltpu.CompilerParams(dimension_semantics=semantics, vmem_limit_bytes=VMEM_LIMIT)


def _rms(x, g):
    return x * lax.rsqrt(jnp.mean(x * x, axis=-1, keepdims=True) + RMS_EPS) * g


U32 = jnp.uint32


def _as_bf16(words):
    return pltpu.bitcast(words, BF16)


def _as_words(x):
    return pltpu.bitcast(x.astype(BF16), U32)


def _pack_kernel(w_ref, o_ref, *, transpose):
    w = w_ref[...]
    o_ref[...] = _as_words(w.T if transpose else w)


def _pack_rows(w, transpose, rows=512):
    N, R, C = w.shape
    out_block, out_map = ((None, C // 2, rows), lambda n, i: (n, 0, i)) if transpose else (
        (None, rows // 2, C), lambda n, i: (n, i, 0))
    out_shape = (N, C // 2, R) if transpose else (N, R // 2, C)
    return pl.pallas_call(
        functools.partial(_pack_kernel, transpose=transpose),
        grid=(N, R // rows),
        in_specs=[pl.BlockSpec((None, rows, C), lambda n, i: (n, i, 0))],
        out_specs=pl.BlockSpec(out_block, out_map),
        out_shape=jax.ShapeDtypeStruct(out_shape, U32),
        compiler_params=_params(("parallel", "parallel")),
    )(w)


def _gelu(x):
    half = 0.5 * x
    return half + half * jnp.tanh(x * (GELU_C + (GELU_C * 0.044715) * (x * x)))


def _inproj_kernel(x_ref, g_ref, w_ref, cos_ref, sa_ref, sb_ref, qkv_ref, p_ref):
    h = _rms(x_ref[...], g_ref[...]).astype(BF16)
    proj = jnp.dot(h, w_ref[...], preferred_element_type=F32)
    c, sa, sb = cos_ref[...], sa_ref[...], sb_ref[...]
    half = ROPE_DIM // 2
    for slab in range(2 * ATTN_WIDTH // LANES):
        t = proj[:, slab * LANES:(slab + 1) * LANES]
        up = pltpu.roll(t, LANES - half, axis=1)
        dn = pltpu.roll(t, half, axis=1)
        rotated = t * c + up * sa + dn * sb
        qkv_ref[slab] = rotated * HEAD_DIM ** -0.5 if slab < ATTN_WIDTH // LANES else rotated
    for slab in range(2 * ATTN_WIDTH // LANES, 3 * ATTN_WIDTH // LANES):
        qkv_ref[slab] = proj[:, slab * LANES:(slab + 1) * LANES]
    p_ref[...] = proj[:, 3 * ATTN_WIDTH:]


def _rope_tables(seq):
    half = ROPE_DIM // 2
    pos = jnp.arange(seq, dtype=F32)
    inv_freq = ROPE_THETA ** (-jnp.arange(0, ROPE_DIM, 2, dtype=F32) / ROPE_DIM)
    ang = pos[:, None] * inv_freq[None, :]
    cos, sin = jnp.cos(ang), jnp.sin(ang)
    pad = HEAD_DIM - ROPE_DIM
    ones = jnp.ones((seq, pad), F32)
    zeros = jnp.zeros((seq, pad), F32)
    zh = jnp.zeros((seq, half), F32)
    c = jnp.concatenate([cos, cos, ones], axis=1)
    sa = jnp.concatenate([-sin, zh, zeros], axis=1)
    sb = jnp.concatenate([zh, sin, zeros], axis=1)
    return tuple(jnp.concatenate([t, t], axis=1) for t in (c, sa, sb))


def _inproj(x2, g, w_in, seq, tm):
    T, D = x2.shape
    n_out = w_in.shape[1]
    pool_w = n_out - 3 * ATTN_WIDTH
    c, sa, sb = _rope_tables(seq)
    spt = seq // tm
    tab = pl.BlockSpec((tm, LANES), lambda i: (i % spt, 0))
    return pl.pallas_call(
        _inproj_kernel,
        grid=(T // tm,),
        in_specs=[pl.BlockSpec((tm, D), lambda i: (i, 0)),
                  pl.BlockSpec((1, D), lambda i: (0, 0)),
                  pl.BlockSpec((D, n_out), lambda i: (0, 0)),
                  tab, tab, tab],
        out_specs=[pl.BlockSpec((3 * ATTN_WIDTH // LANES, tm, LANES), lambda i: (0, i, 0)),
                   pl.BlockSpec((tm, pool_w), lambda i: (i, 0))],
        out_shape=[jax.ShapeDtypeStruct((3 * ATTN_WIDTH // LANES, T, LANES), F32),
                   jax.ShapeDtypeStruct((T, pool_w), F32)],
        compiler_params=_params(("parallel",)),
    )(x2, g.reshape(1, D), w_in.astype(BF16), c, sa, sb)


def _attn_kernel(q_ref, k_ref, v_ref, a_ref, o_scr, lse_scr, *, seq):
    lane = lax.broadcasted_iota(jnp.int32, (QUERY_BLOCK, LANES), 1)
    first_head = lane < HEAD_DIM
    head_masks = (first_head, jnp.logical_not(first_head))

    for pat, dil in enumerate(DILATIONS):
        length = seq // dil
        kw = min(QUERY_BLOCK + 2 * ATTN_HALF, length)
        blocks = length // QUERY_BLOCK
        ones = jnp.ones((kw, LANES), BF16)

        def group(g, carry, dil=dil, length=length, kw=kw, blocks=blocks, pat=pat, ones=ones):
            loaded = []
            for j in range(BLOCK_UNROLL):
                it = g * BLOCK_UNROLL + j
                res, qb = it // blocks, it % blocks
                q0 = qb * QUERY_BLOCK
                start = jnp.clip(q0 - ATTN_HALF, 0, length - kw)

                def rows(ref, first, count, res=res):
                    return ref[pl.ds(res + dil * first, count, stride=dil), :]

                qpos = q0 + lax.broadcasted_iota(jnp.int32, (QUERY_BLOCK, kw), 0)
                kpos = start + lax.broadcasted_iota(jnp.int32, (QUERY_BLOCK, kw), 1)
                loaded.append((rows(q_ref, q0, QUERY_BLOCK).astype(BF16), rows(k_ref, start, kw).astype(BF16),
                               rows(v_ref, start, kw).astype(BF16), jnp.abs(kpos - qpos) <= ATTN_HALF,
                               pl.ds(res + dil * q0, QUERY_BLOCK, stride=dil)))
            chains = [(j, mask) for j in range(BLOCK_UNROLL) for mask in head_masks]
            scores = []
            for j, mask in chains:
                qp, kp, _, valid, _ = loaded[j]
                qm = jnp.where(mask, qp, jnp.zeros_like(qp))
                s = lax.dot_general(qm, kp, (((1,), (1,)), ((), ())), preferred_element_type=F32)
                scores.append(jnp.where(valid, s, NEG_INF))
            tops = [jnp.max(s, axis=-1, keepdims=True) for s in scores]
            probs = [jnp.exp(s - m).astype(BF16) for s, m in zip(scores, tops)]
            totals = [jnp.dot(p, ones, preferred_element_type=F32) for p in probs]
            mixes = [jnp.dot(p, loaded[j][2], preferred_element_type=F32) for p, (j, _) in zip(probs, chains)]
            outs = [o / t for o, t in zip(mixes, totals)]
            lses = [m + jnp.log(t) for m, t in zip(tops, totals)]
            for j in range(BLOCK_UNROLL):
                dst = loaded[j][4]
                o_scr[pat, dst, :] = jnp.where(first_head, outs[2 * j], outs[2 * j + 1])
                lse_scr[pat, dst, :] = jnp.where(first_head, lses[2 * j], lses[2 * j + 1])
            return carry

        assert (dil * blocks) % BLOCK_UNROLL == 0
        lax.fori_loop(0, dil * blocks // BLOCK_UNROLL, group, 0)

    la, lb, lc = lse_scr[0], lse_scr[1], lse_scr[2]
    m = jnp.maximum(jnp.maximum(la, lb), lc)
    ea, eb, ec = jnp.exp(la - m), jnp.exp(lb - m), jnp.exp(lc - m)
    a_ref[...] = ((o_scr[0] * ea + o_scr[1] * eb + o_scr[2] * ec) / (ea + eb + ec)).astype(BF16)


def _attention(qkv, batch, seq):
    pairs = ATTN_WIDTH // LANES

    def part(j):
        return pl.BlockSpec((None, seq, LANES), lambda b, hp: (j * pairs + hp, b, 0))

    return pl.pallas_call(
        functools.partial(_attn_kernel, seq=seq),
        grid=(batch, pairs),
        in_specs=[part(0), part(1), part(2)],
        out_specs=pl.BlockSpec((seq, LANES), lambda b, hp: (b, hp)),
        out_shape=jax.ShapeDtypeStruct((batch * seq, ATTN_WIDTH), BF16),
        scratch_shapes=[pltpu.VMEM((len(DILATIONS), seq, LANES), F32),
                        pltpu.VMEM((len(DILATIONS), seq, LANES), F32)],
        compiler_params=_params(("parallel", "parallel")),
    )(qkv, qkv, qkv)


def _mix0_kernel(a_ref, pc_ref, pp_ref, pn_ref, x_ref, wout_ref, pw_ref, ps_ref,
                 out_ref, ext_ref, *, seq, tm):
    ti = pl.program_id(0) % (seq // tm)
    acc = jnp.dot(a_ref[...], wout_ref[:ATTN_WIDTH, :], preferred_element_type=F32)

    ext_ref[:POOL_HALO, :] = jnp.where(ti > 0, pp_ref[...], 0.0)
    ext_ref[POOL_HALO:POOL_HALO + tm, :] = pc_ref[...]
    ext_ref[POOL_HALO + tm:, :] = jnp.where(ti < seq // tm - 1, pn_ref[...], 0.0)
    pos = ti * tm + lax.broadcasted_iota(jnp.int32, (tm, 1), 0)
    for g, win in enumerate(POOL_WINDOWS):
        cols = slice(g * LANES, (g + 1) * LANES)
        before, after = win // 2, win - win // 2
        tot = ext_ref[POOL_HALO - before:POOL_HALO - before + tm, cols]
        for d in range(-before + 1, after):
            tot = tot + ext_ref[POOL_HALO + d:POOL_HALO + d + tm, cols]
        cnt = (jnp.minimum(pos + after, seq) - jnp.maximum(pos - before, 0)).astype(F32)
        diff = tot / cnt - pc_ref[:, cols]
        mixed = jnp.dot(diff.astype(BF16), pw_ref[g], preferred_element_type=F32) * ps_ref[:, cols]
        acc = acc + jnp.dot(mixed.astype(BF16),
                            wout_ref[ATTN_WIDTH + g * LANES:ATTN_WIDTH + (g + 1) * LANES, :],
                            preferred_element_type=F32)
    out_ref[...] = x_ref[...] + acc


def _mix0(a_out, p_in, x2, w_out, pool_w, pool_scale, seq, tm):
    T, D = x2.shape
    pw = p_in.shape[1]
    hb = tm // POOL_HALO
    last = T // POOL_HALO - 1
    row = lambda w: pl.BlockSpec((tm, w), lambda i: (i, 0))
    full = lambda s: pl.BlockSpec(s, lambda i: (0,) * len(s))
    return pl.pallas_call(
        functools.partial(_mix0_kernel, seq=seq, tm=tm),
        grid=(T // tm,),
        in_specs=[
            row(ATTN_WIDTH), row(pw),
            pl.BlockSpec((POOL_HALO, pw), lambda i: (jnp.maximum(i * hb - 1, 0), 0)),
            pl.BlockSpec((POOL_HALO, pw), lambda i: (jnp.minimum((i + 1) * hb, last), 0)),
            row(D), full(w_out.shape), full(pool_w.shape), full((1, pw))],
        out_specs=row(D),
        out_shape=jax.ShapeDtypeStruct((T, D), F32),
        scratch_shapes=[pltpu.VMEM((tm + 2 * POOL_HALO, pw), F32)],
        compiler_params=_params(("parallel",)),
    )(a_out, p_in, p_in, p_in, x2, w_out.astype(BF16), pool_w.astype(BF16),
      pool_scale.reshape(1, pw))


def _sgu_kernel(x_ref, g_ref, win_ref, cn_ref, ws_ref, bs_ref, wout_ref, out_ref, gated_ref, *, tm):
    x = x_ref[...]
    width = wout_ref.shape[0]
    h = _rms(x, g_ref[...]).astype(BF16)
    u = _gelu(jnp.dot(h, win_ref[:, :width], preferred_element_type=F32))
    v = _gelu(jnp.dot(h, win_ref[:, width:], preferred_element_type=F32))
    vb = _rms(v, cn_ref[...]).astype(BF16)
    for n in range(tm // SGU_CHUNK):
        rows = slice(n * SGU_CHUNK, (n + 1) * SGU_CHUNK)
        for g in range(SGU_GROUPS):
            cols = slice(g * LANES, (g + 1) * LANES)
            mixed = jnp.dot(ws_ref[g], vb[rows, cols], preferred_element_type=F32) + bs_ref[:, cols]
            gated_ref[rows, cols] = (u[rows, cols] * mixed).astype(BF16)
    out_ref[...] = x + jnp.dot(gated_ref[...], wout_ref[...], preferred_element_type=F32)


def _sgu(x2, g, w_in, c_norm, w_s, b_s, w_out, tm):
    T, D = x2.shape
    width = w_out.shape[0]
    bias = jnp.repeat(b_s.T, width // SGU_GROUPS, axis=1)
    full = lambda s: pl.BlockSpec(s, lambda i: (0,) * len(s))
    return pl.pallas_call(
        functools.partial(_sgu_kernel, tm=tm),
        grid=(T // tm,),
        in_specs=[pl.BlockSpec((tm, D), lambda i: (i, 0)), full((1, D)), full(w_in.shape),
                  full((1, width)), full(w_s.shape), full(bias.shape), full(w_out.shape)],
        out_specs=pl.BlockSpec((tm, D), lambda i: (i, 0)),
        out_shape=jax.ShapeDtypeStruct((T, D), F32),
        scratch_shapes=[pltpu.VMEM((tm, width), BF16)],
        compiler_params=_params(("parallel",)),
    )(x2, g.reshape(1, D), w_in.astype(BF16), c_norm.reshape(1, width), w_s.astype(BF16), bias,
      w_out.astype(BF16))


def _cmpx(v, i, l, descending):
    hi, lo = jnp.maximum(v[i], v[l]), jnp.minimum(v[i], v[l])
    v[i], v[l] = (hi, lo) if descending else (lo, hi)


def _bitonic_sort_desc(v):
    v = list(v)
    n = len(v)
    k = 2
    while k <= n:
        j = k // 2
        while j >= 1:
            for i in range(n):
                l = i ^ j
                if l > i:
                    _cmpx(v, i, l, (i & k) == 0)
            j //= 2
        k *= 2
    return v


def _merge_top(a, b):
    n = len(a)
    v = [jnp.maximum(a[i], b[n - 1 - i]) for i in range(n)]
    j = n // 2
    while j >= 1:
        for i in range(n):
            if (i & j) == 0:
                _cmpx(v, i, i + j, True)
        j //= 2
    return v


def _top_sorted(vals, k):
    groups = [_bitonic_sort_desc(vals[i:i + k]) for i in range(0, len(vals), k)]
    while len(groups) > 1:
        groups = [_merge_top(groups[i], groups[i + 1]) for i in range(0, len(groups), 2)]
    return groups[0]


def _count_reached(s, t):
    assert len(t) == 16
    c16 = s >= t[15]
    c8 = s >= t[7]
    c4 = s >= jnp.where(c8, t[11], t[3])
    c2 = s >= jnp.where(c8, jnp.where(c4, t[13], t[9]), jnp.where(c4, t[5], t[1]))
    quarter = [jnp.where(c2, t[4 * q + 2], t[4 * q]) for q in range(4)]
    c1 = s >= jnp.where(c8, jnp.where(c4, quarter[3], quarter[2]), jnp.where(c4, quarter[1], quarter[0]))
    return (jnp.where(c8, 8.0, 0.0) + jnp.where(c4, 4.0, 0.0) + jnp.where(c2, 2.0, 0.0)
            + jnp.where(c1, 1.0, 0.0) + jnp.where(c16, 1.0, 0.0))


def _count(preds):
    tot = jnp.where(preds[0], 1.0, 0.0)
    for p in preds[1:]:
        tot = tot + jnp.where(p, 1.0, 0.0)
    return tot


def _twin_bf16(v):
    bits = pltpu.bitcast(v.astype(BF16).astype(F32), U32)
    return bits | (bits >> 16)


def _peer_gate_kernel(x_ref, g_ref, wq_ref, keys_ref, ht_ref, r2_ref, e2_ref, ne_ref,
                      km0_ref, km1_ref, om_ref, *, tg, heads, n_work):
    s = pl.program_id(0)

    @pl.when((s % heads == 0) & (s < n_work))
    def _():
        ht_ref[...] = _as_words(_rms(x_ref[...], g_ref[...]).T)

    @pl.when(s == 0)
    def _():
        km1_ref[...] = jnp.zeros_like(km1_ref)

    args = (wq_ref, keys_ref, ht_ref, r2_ref, e2_ref, ne_ref, om_ref, tg)
    pl.when(s % 2 == 0)(lambda: _gate_step(km0_ref, km1_ref, *args))
    pl.when(s % 2 == 1)(lambda: _gate_step(km1_ref, km0_ref, *args))


def _gate_step(km_new, km_old, wq_ref, keys_ref, ht_ref, r2_ref, e2_ref, ne_ref, om_ref, tg):
    K = PEER_TOPK
    nk = PEER_N_KEYS
    chunks = tg // LANES
    half_dim = wq_ref.shape[0] // 2

    def score():
        qt = jnp.dot(wq_ref[...], _as_bf16(ht_ref[...]), preferred_element_type=F32)
        for side in range(2):
            sc = jnp.dot(keys_ref[side], qt[side * half_dim:(side + 1) * half_dim].astype(BF16),
                         preferred_element_type=F32)
            for c in range(chunks):
                km_new[side, pl.ds(c, nk, stride=chunks), :] = sc[:, c * LANES:(c + 1) * LANES]

    def keyrows(k):
        return slice(k * chunks, (k + 1) * chunks)

    s1 = [km_old[0, keyrows(k), :] for k in range(nk)]
    s2 = [km_old[1, keyrows(k), :] for k in range(nk)]
    A = _top_sorted(s1, K)
    B = _top_sorted(s2, K)
    score()

    reach = [K // (i + 1) for i in range(K)]
    cand = [[A[i] + B[j] for j in range(reach[i])] for i in range(K)]
    ninf = jnp.full_like(A[0], -jnp.inf)
    rest = [c for row in cand[1:] for c in row]
    rest = rest + [ninf] * (-len(rest) % K)
    top = cand[0]
    for i in range(0, len(rest), K):
        top = _merge_top(top, _bitonic_sort_desc(rest[i:i + K]))
    thr = top[K - 1]

    gt = [_count([c > thr for c in row]) for row in cand]
    eq = [_count([c == thr for c in row]) for row in cand]
    total_gt = functools.reduce(lambda a, b: a + b, gt)
    room = float(K) - total_gt
    n = []
    for i in range(K):
        n.append(gt[i] + jnp.clip(room, 0.0, eq[i]))
        room = room - eq[i]

    e1 = [jnp.exp(a - A[0]) for a in A]
    e2 = [jnp.exp(b - B[0]) for b in B]
    z = jnp.zeros_like(A[0])
    for i in range(K):
        inner = jnp.zeros_like(z)
        for j in range(reach[i]):
            inner = inner + jnp.where(n[i] > float(j), e2[j], 0.0)
        z = z + e1[i] * inner
    inv_z = 1.0 / z

    pinf = jnp.full_like(A[0], jnp.inf)
    t = []
    for m in range(1, K + 1):
        tm_ = pinf
        for i in range(K):
            tm_ = jnp.minimum(tm_, jnp.where(n[i] >= float(m), A[i], pinf))
        t.append(tm_)

    b_up = B[::-1]
    for k in range(nk):
        om_ref[0, keyrows(k), :] = _count_reached(s1[k], t)
        om_ref[1, keyrows(k), :] = jnp.exp(s1[k] - A[0]) * inv_z
        om_ref[2, keyrows(k), :] = float(K) - _count_reached(s2[k], b_up)
        om_ref[3, keyrows(k), :] = jnp.exp(s2[k] - B[0])

    for c in range(chunks):
        cols = slice(c * LANES, (c + 1) * LANES)
        ne_ref[0, c] = _twin_bf16(om_ref[0, pl.ds(c, nk, stride=chunks), :])
        ne_ref[1, c] = _twin_bf16(om_ref[1, pl.ds(c, nk, stride=chunks), :])
        r2_ref[:, cols] = _as_words(om_ref[2, pl.ds(c, nk, stride=chunks), :])
        e2_ref[:, cols] = _as_words(om_ref[3, pl.ds(c, nk, stride=chunks), :])


def _peer_gates(x2, g, w_q, sub_keys):
    T, D = x2.shape
    tg = VREG_TOKENS
    heads, _, nk, half_dim = sub_keys.shape
    wq_t = w_q.T.astype(BF16)
    n_work = (T // tg) * heads

    def item(lag):
        def split(s):
            w = jnp.clip(s - lag, 0, n_work - 1)
            return w // heads, w % heads
        return split

    scored, chosen = item(0), item(1)
    return pl.pallas_call(
        functools.partial(_peer_gate_kernel, tg=tg, heads=heads, n_work=n_work),
        grid=(n_work + 1,),
        in_specs=[pl.BlockSpec((tg, D), lambda s: (scored(s)[0], 0)),
                  pl.BlockSpec((1, D), lambda s: (0, 0)),
                  pl.BlockSpec((2 * half_dim, D), lambda s: (scored(s)[1], 0)),
                  pl.BlockSpec((None, 2, nk, half_dim), lambda s: (scored(s)[1], 0, 0, 0))],
        out_specs=[pl.BlockSpec((D // 2, tg), lambda s: (0, scored(s)[0])),
                   pl.BlockSpec((nk // 2, tg), lambda s: (chosen(s)[1], chosen(s)[0])),
                   pl.BlockSpec((nk // 2, tg), lambda s: (chosen(s)[1], chosen(s)[0])),
                   pl.BlockSpec((None, 2, tg // LANES, nk, LANES),
                                lambda s: (chosen(s)[1], 0, chosen(s)[0], 0, 0))],
        out_shape=[jax.ShapeDtypeStruct((D // 2, T), U32),
                   jax.ShapeDtypeStruct((heads * nk // 2, T), U32),
                   jax.ShapeDtypeStruct((heads * nk // 2, T), U32),
                   jax.ShapeDtypeStruct((heads, 2, T // LANES, nk, LANES), U32)],
        scratch_shapes=[pltpu.VMEM((2, nk * SUBLANES, LANES), F32),
                        pltpu.VMEM((2, nk * SUBLANES, LANES), F32),
                        pltpu.VMEM((4, nk * SUBLANES, LANES), F32)],
        compiler_params=_params(("arbitrary",)),
    )(x2, g.reshape(1, D), wq_t, sub_keys.astype(BF16))


KEY_BLOCK = 4
PIECE_ROWS = 256


def _gate_hidden(hid_ref, ga_ref, r2_ref, e2_ref, ne_ref, first_key, c, al0):
    nk = PEER_N_KEYS
    tiles = range(0, nk, BF16_ROWS)
    cols = slice(c * LANES, (c + 1) * LANES)

    def row_tile(h, q, al):
        return _as_bf16(ne_ref[h, q, c, pl.ds(first_key + al, SUBLANES, stride=0), :])

    gates = [[None for _ in tiles] for _ in range(KEY_BLOCK)]
    for h in range(PEER_HEADS):
        counts = [row_tile(h, 0, al0 + k) for k in range(KEY_BLOCK)]
        factors = [row_tile(h, 1, al0 + k) for k in range(KEY_BLOCK)]
        for ti, r0 in enumerate(tiles):
            hrows = slice((h * nk + r0) // 2, (h * nk + r0) // 2 + SUBLANES)
            rank, e2 = _as_bf16(r2_ref[hrows, cols]), _as_bf16(e2_ref[hrows, cols])
            for k in range(KEY_BLOCK):
                term = jnp.where(rank < counts[k], e2, jnp.zeros_like(e2)) * factors[k]
                gates[k][ti] = term if gates[k][ti] is None else gates[k][ti] + term
    for k in range(KEY_BLOCK):
        for ti, r0 in enumerate(tiles):
            rows = slice((al0 + k) * nk + r0, (al0 + k) * nk + r0 + BF16_ROWS)
            ga_ref[rows, cols] = gates[k][ti] * _gelu(hid_ref[rows, cols].astype(BF16))


def _peer_expert_kernel(ht_ref, wd_ref, wu_ref, r2_ref, e2_ref, ne_ref, x_ref, gf_ref, out_ref,
                        acc_ref, hid0_ref, hid1_ref, ga0_ref, ga1_ref, *, final_norm, n_blocks, n_work):
    s = pl.program_id(0)
    eb = hid0_ref.shape[0]
    gate_item = jnp.clip(s - 1, 0, n_work - 1)
    up_block = jnp.clip(s - 2, 0, n_work - 1) % n_blocks
    first_key = (gate_item % n_blocks) * (eb // PEER_N_KEYS)

    @pl.when(s == 0)
    def _():
        hid1_ref[...] = jnp.zeros_like(hid1_ref)
        ga0_ref[...] = jnp.zeros_like(ga0_ref)

    @pl.when(up_block == 0)
    def _():
        acc_ref[...] = jnp.zeros_like(acc_ref)

    def stages(hid_new, hid_old, ga_new, ga_old):
        tm = hid_new.shape[1]
        d_model = acc_ref.shape[0]

        def up(n, m):
            cols = slice(n * MXU_WIDTH, (n + 1) * MXU_WIDTH)
            rows = slice(m * PIECE_ROWS, (m + 1) * PIECE_ROWS)
            words = slice(m * PIECE_ROWS // 2, (m + 1) * PIECE_ROWS // 2)
            acc_ref[rows, cols] += jnp.dot(_as_bf16(wu_ref[words, :]), ga_old[:, cols],
                                           preferred_element_type=F32)

        def down(n, m):
            cols = slice(n * MXU_WIDTH, (n + 1) * MXU_WIDTH)
            rows = slice(m * PIECE_ROWS, (m + 1) * PIECE_ROWS)
            words = slice(m * PIECE_ROWS // 2, (m + 1) * PIECE_ROWS // 2)
            hid_new[rows, cols] = jnp.dot(_as_bf16(wd_ref[words, :]), _as_bf16(ht_ref[:, cols]),
                                          preferred_element_type=F32)

        ups, downs = d_model // PIECE_ROWS, eb // PIECE_ROWS
        assert downs % ups == 0
        mxu = [functools.partial(f, n, m) for n in range(tm // MXU_WIDTH) for u in range(ups)
               for f, m in [(up, u)] + [(down, u * (downs // ups) + d) for d in range(downs // ups)]]
        vpu = [functools.partial(_gate_hidden, hid_old, ga_new, r2_ref, e2_ref, ne_ref, first_key, c, al0)
               for al0 in range(0, eb // PEER_N_KEYS, KEY_BLOCK) for c in range(tm // LANES)]
        done = 0
        for i, piece in enumerate(mxu):
            piece()
            upto = -(-(i + 1) * len(vpu) // len(mxu))
            for g in vpu[done:upto]:
                g()
            done = upto

    pl.when(s % 2 == 0)(lambda: stages(hid0_ref, hid1_ref, ga1_ref, ga0_ref))
    pl.when(s % 2 == 1)(lambda: stages(hid1_ref, hid0_ref, ga0_ref, ga1_ref))

    @pl.when((s >= 2) & (up_block == n_blocks - 1))
    def _():
        y = x_ref[...] + acc_ref[...].T
        out_ref[...] = _rms(y, gf_ref[...]) if final_norm else y


def _peer_experts(ht, r2, e2, ne, x2, wd_words, wu_words, layer, g_final, final_norm, tm, eb):
    T, D = x2.shape
    n_exp = wu_words.shape[2]
    heads, _, _, nk, _ = ne.shape
    n_blocks = n_exp // eb
    n_work = (T // tm) * n_blocks

    def item(lag):
        def split(s):
            w = jnp.clip(s - lag, 0, n_work - 1)
            return w // n_blocks, w % n_blocks
        return split

    down, gate, up = item(0), item(1), item(2)
    return pl.pallas_call(
        functools.partial(_peer_expert_kernel, final_norm=final_norm, n_blocks=n_blocks, n_work=n_work),
        grid=(n_work + 2,),
        in_specs=[pl.BlockSpec((D // 2, tm), lambda s: (0, down(s)[0])),
                  pl.BlockSpec((None, eb // 2, D), lambda s: (layer, down(s)[1], 0)),
                  pl.BlockSpec((None, D // 2, eb), lambda s: (layer, 0, up(s)[1])),
                  pl.BlockSpec((heads * nk // 2, tm), lambda s: (0, gate(s)[0])),
                  pl.BlockSpec((heads * nk // 2, tm), lambda s: (0, gate(s)[0])),
                  pl.BlockSpec((heads, 2, tm // LANES, nk, LANES), lambda s: (0, 0, gate(s)[0], 0, 0)),
                  pl.BlockSpec((tm, D), lambda s: (up(s)[0], 0)),
                  pl.BlockSpec((1, D), lambda s: (0, 0))],
        out_specs=pl.BlockSpec((tm, D), lambda s: (up(s)[0], 0)),
        out_shape=jax.ShapeDtypeStruct((T, D), F32),
        scratch_shapes=[pltpu.VMEM((D, tm), F32), pltpu.VMEM((eb, tm), F32), pltpu.VMEM((eb, tm), F32),
                        pltpu.VMEM((eb, tm), BF16), pltpu.VMEM((eb, tm), BF16)],
        compiler_params=_params(("arbitrary",)),
    )(ht, wd_words, wu_words, r2, e2, ne, x2, g_final.reshape(1, D))


def kernel(x, norm_mix, norm_ffn, norm_final, ab_w_in, ab_w_out, pool_w, pool_scale, c_w_in,
           c_norm, c_w_s, c_b_s, c_w_out, peer_w_q, peer_sub_keys, peer_w_down, peer_w_up):
    B, S, D = x.shape
    depth = norm_mix.shape[0]
    tm = 512
    eb = 2048
    assert S % tm == 0 and (B * S) % VREG_TOKENS == 0
    assert all(S % (d * QUERY_BLOCK) == 0 for d in DILATIONS)
    x2 = x.reshape(B * S, D)
    wd_words = _pack_rows(peer_w_down, transpose=False)
    wu_words = _pack_rows(peer_w_up, transpose=True)
    for layer in range(depth):
        j = layer // 2
        if layer % 2 == 0:
            qkv, p_in = _inproj(x2, norm_mix[layer], ab_w_in[j], S, tm)
            a_out = _attention(qkv, B, S)
            x2 = _mix0(a_out, p_in, x2, ab_w_out[j], pool_w[j], pool_scale[j], S, tm)
        else:
            x2 = _sgu(x2, norm_mix[layer], c_w_in[j], c_norm[j], c_w_s[j], c_b_s[j], c_w_out[j], tm)
        ht, r2, e2, ne = _peer_gates(x2, norm_ffn[layer], peer_w_q[layer], peer_sub_keys[layer])
        x2 = _peer_experts(ht, r2, e2, ne, x2, wd_words, wu_words, layer, norm_final,
                           layer == depth - 1, tm, eb)
    return x2.reshape(B, S, D)
```

```python
import functools
import math

import jax
import jax.numpy as jnp
from jax import lax
from jax.experimental import pallas as pl
from jax.experimental.pallas import tpu as pltpu

F32 = jnp.float32
BF16 = jnp.bfloat16

LANES = 128
SUBLANES = 8
BF16_ROWS = 2 * SUBLANES
MXU_WIDTH = 256
VREG_TOKENS = SUBLANES * LANES
VMEM_LIMIT = 56 * 1024 * 1024

ATTN_HEADS = 8
HEAD_DIM = 64
ATTN_WIDTH = ATTN_HEADS * HEAD_DIM
DILATIONS = (1, 4, 16)
ATTN_HALF = 64
QUERY_BLOCK = 128
BLOCK_UNROLL = 8
ROPE_THETA = 500000.0
ROPE_DIM = HEAD_DIM // 4
POOL_WINDOWS = (2, 4, 8, 16)
POOL_HALO = 8
SGU_CHUNK = 128
SGU_GROUPS = 8
PEER_HEADS = 8
PEER_N_KEYS = 128
PEER_TOPK = 16
RMS_EPS = 1e-6
NEG_INF = -1e30
GELU_C = math.sqrt(2.0 / math.pi)
GELU_CUBIC = 0.044715


def _params(semantics):
    return pltpu.CompilerParams(dimension_semantics=semantics, vmem_limit_bytes=VMEM_LIMIT)


def _rms(x, g):
    return x * lax.rsqrt(jnp.mean(x * x, axis=-1, keepdims=True) + RMS_EPS) * g


U32 = jnp.uint32


def _as_bf16(words):
    return pltpu.bitcast(words, BF16)


def _as_words(x):
    return pltpu.bitcast(x.astype(BF16), U32)


def _pack_kernel(w_ref, o_ref, *, transpose):
    w = w_ref[...]
    o_ref[...] = _as_words(w.T if transpose else w)


def _pack_rows(w, transpose, rows=512):
    N, R, C = w.shape
    out_block, out_map = ((None, C // 2, rows), lambda n, i: (n, 0, i)) if transpose else (
        (None, rows // 2, C), lambda n, i: (n, i, 0))
    out_shape = (N, C // 2, R) if transpose else (N, R // 2, C)
    return pl.pallas_call(
        functools.partial(_pack_kernel, transpose=transpose),
        grid=(N, R // rows),
        in_specs=[pl.BlockSpec((None, rows, C), lambda n, i: (n, i, 0))],
        out_specs=pl.BlockSpec(out_block, out_map),
        out_shape=jax.ShapeDtypeStruct(out_shape, U32),
        compiler_params=_params(("parallel", "parallel")),
    )(w)


def _gelu(x):
    half = 0.5 * x
    return half + half * jnp.tanh(x * (GELU_C + (GELU_C * GELU_CUBIC) * (x * x)))


def _inproj_kernel(x_ref, g_ref, w_ref, cos_ref, sa_ref, sb_ref, qkv_ref, p_ref):
    h = _rms(x_ref[...], g_ref[...]).astype(BF16)
    proj = jnp.dot(h, w_ref[...], preferred_element_type=F32)
    c, sa, sb = cos_ref[...], sa_ref[...], sb_ref[...]
    half = ROPE_DIM // 2
    for slab in range(2 * ATTN_WIDTH // LANES):
        t = proj[:, slab * LANES:(slab + 1) * LANES]
        up = pltpu.roll(t, LANES - half, axis=1)
        dn = pltpu.roll(t, half, axis=1)
        rotated = t * c + up * sa + dn * sb
        qkv_ref[slab] = rotated * HEAD_DIM ** -0.5 if slab < ATTN_WIDTH // LANES else rotated
    for slab in range(2 * ATTN_WIDTH // LANES, 3 * ATTN_WIDTH // LANES):
        qkv_ref[slab] = proj[:, slab * LANES:(slab + 1) * LANES]
    p_ref[...] = proj[:, 3 * ATTN_WIDTH:]


def _rope_tables(seq):
    half = ROPE_DIM // 2
    pos = jnp.arange(seq, dtype=F32)
    inv_freq = ROPE_THETA ** (-jnp.arange(0, ROPE_DIM, 2, dtype=F32) / ROPE_DIM)
    ang = pos[:, None] * inv_freq[None, :]
    cos, sin = jnp.cos(ang), jnp.sin(ang)
    pad = HEAD_DIM - ROPE_DIM
    ones = jnp.ones((seq, pad), F32)
    zeros = jnp.zeros((seq, pad), F32)
    zh = jnp.zeros((seq, half), F32)
    c = jnp.concatenate([cos, cos, ones], axis=1)
    sa = jnp.concatenate([-sin, zh, zeros], axis=1)
    sb = jnp.concatenate([zh, sin, zeros], axis=1)
    return tuple(jnp.concatenate([t, t], axis=1) for t in (c, sa, sb))


def _inproj(x2, g, w_in, seq, tm):
    T, D = x2.shape
    n_out = w_in.shape[1]
    pool_w = n_out - 3 * ATTN_WIDTH
    c, sa, sb = _rope_tables(seq)
    spt = seq // tm
    tab = pl.BlockSpec((tm, LANES), lambda i: (i % spt, 0))
    return pl.pallas_call(
        _inproj_kernel,
        grid=(T // tm,),
        in_specs=[pl.BlockSpec((tm, D), lambda i: (i, 0)),
                  pl.BlockSpec((1, D), lambda i: (0, 0)),
                  pl.BlockSpec((D, n_out), lambda i: (0, 0)),
                  tab, tab, tab],
        out_specs=[pl.BlockSpec((3 * ATTN_WIDTH // LANES, tm, LANES), lambda i: (0, i, 0)),
                   pl.BlockSpec((tm, pool_w), lambda i: (i, 0))],
        out_shape=[jax.ShapeDtypeStruct((3 * ATTN_WIDTH // LANES, T, LANES), F32),
                   jax.ShapeDtypeStruct((T, pool_w), F32)],
        compiler_params=_params(("parallel",)),
    )(x2, g.reshape(1, D), w_in.astype(BF16), c, sa, sb)


def _attn_kernel(q_ref, k_ref, v_ref, a_ref, o_scr, lse_scr, *, seq):
    lane = lax.broadcasted_iota(jnp.int32, (QUERY_BLOCK, LANES), 1)
    first_head = lane < HEAD_DIM
    head_masks = (first_head, jnp.logical_not(first_head))

    for pat, dil in enumerate(DILATIONS):
        length = seq // dil
        kw = min(QUERY_BLOCK + 2 * ATTN_HALF, length)
        blocks = length // QUERY_BLOCK
        ones = jnp.ones((kw, LANES), BF16)

        def group(g, carry, dil=dil, length=length, kw=kw, blocks=blocks, pat=pat, ones=ones):
            loaded = []
            for j in range(BLOCK_UNROLL):
                it = g * BLOCK_UNROLL + j
                res, qb = it // blocks, it % blocks
                q0 = qb * QUERY_BLOCK
                start = jnp.clip(q0 - ATTN_HALF, 0, length - kw)

                def rows(ref, first, count, res=res):
                    return ref[pl.ds(res + dil * first, count, stride=dil), :]

                qpos = q0 + lax.broadcasted_iota(jnp.int32, (QUERY_BLOCK, kw), 0)
                kpos = start + lax.broadcasted_iota(jnp.int32, (QUERY_BLOCK, kw), 1)
                loaded.append((rows(q_ref, q0, QUERY_BLOCK).astype(BF16), rows(k_ref, start, kw).astype(BF16),
                               rows(v_ref, start, kw).astype(BF16), jnp.abs(kpos - qpos) <= ATTN_HALF,
                               pl.ds(res + dil * q0, QUERY_BLOCK, stride=dil)))
            chains = [(j, mask) for j in range(BLOCK_UNROLL) for mask in head_masks]
            scores = []
            for j, mask in chains:
                qp, kp, _, valid, _ = loaded[j]
                qm = jnp.where(mask, qp, jnp.zeros_like(qp))
                s = lax.dot_general(qm, kp, (((1,), (1,)), ((), ())), preferred_element_type=F32)
                scores.append(jnp.where(valid, s, NEG_INF))
            tops = [jnp.max(s, axis=-1, keepdims=True) for s in scores]
            probs = [jnp.exp(s - m).astype(BF16) for s, m in zip(scores, tops)]
            totals = [jnp.dot(p, ones, preferred_element_type=F32) for p in probs]
            mixes = [jnp.dot(p, loaded[j][2], preferred_element_type=F32) for p, (j, _) in zip(probs, chains)]
            outs = [o / t for o, t in zip(mixes, totals)]
            lses = [m + jnp.log(t) for m, t in zip(tops, totals)]
            for j in range(BLOCK_UNROLL):
                dst = loaded[j][4]
                o_scr[pat, dst, :] = jnp.where(first_head, outs[2 * j], outs[2 * j + 1])
                lse_scr[pat, dst, :] = jnp.where(first_head, lses[2 * j], lses[2 * j + 1])
            return carry

        assert (dil * blocks) % BLOCK_UNROLL == 0
        lax.fori_loop(0, dil * blocks // BLOCK_UNROLL, group, 0)

    la, lb, lc = lse_scr[0], lse_scr[1], lse_scr[2]
    m = jnp.maximum(jnp.maximum(la, lb), lc)
    ea, eb, ec = jnp.exp(la - m), jnp.exp(lb - m), jnp.exp(lc - m)
    a_ref[...] = ((o_scr[0] * ea + o_scr[1] * eb + o_scr[2] * ec) / (ea + eb + ec)).astype(BF16)


def _attention(qkv, batch, seq):
    pairs = ATTN_WIDTH // LANES

    def part(j):
        return pl.BlockSpec((None, seq, LANES), lambda b, hp: (j * pairs + hp, b, 0))

    return pl.pallas_call(
        functools.partial(_attn_kernel, seq=seq),
        grid=(batch, pairs),
        in_specs=[part(0), part(1), part(2)],
        out_specs=pl.BlockSpec((seq, LANES), lambda b, hp: (b, hp)),
        out_shape=jax.ShapeDtypeStruct((batch * seq, ATTN_WIDTH), BF16),
        scratch_shapes=[pltpu.VMEM((len(DILATIONS), seq, LANES), F32),
                        pltpu.VMEM((len(DILATIONS), seq, LANES), F32)],
        compiler_params=_params(("parallel", "parallel")),
    )(qkv, qkv, qkv)


def _mix0_kernel(a_ref, pc_ref, pp_ref, pn_ref, x_ref, wout_ref, pw_ref, ps_ref,
                 out_ref, ext_ref, *, seq, tm):
    ti = pl.program_id(0) % (seq // tm)
    acc = jnp.dot(a_ref[...], wout_ref[:ATTN_WIDTH, :], preferred_element_type=F32)

    ext_ref[:POOL_HALO, :] = jnp.where(ti > 0, pp_ref[...], 0.0)
    ext_ref[POOL_HALO:POOL_HALO + tm, :] = pc_ref[...]
    ext_ref[POOL_HALO + tm:, :] = jnp.where(ti < seq // tm - 1, pn_ref[...], 0.0)
    pos = ti * tm + lax.broadcasted_iota(jnp.int32, (tm, 1), 0)
    for g, win in enumerate(POOL_WINDOWS):
        cols = slice(g * LANES, (g + 1) * LANES)
        before, after = win // 2, win - win // 2
        tot = ext_ref[POOL_HALO - before:POOL_HALO - before + tm, cols]
        for d in range(-before + 1, after):
            tot = tot + ext_ref[POOL_HALO + d:POOL_HALO + d + tm, cols]
        cnt = (jnp.minimum(pos + after, seq) - jnp.maximum(pos - before, 0)).astype(F32)
        diff = tot / cnt - pc_ref[:, cols]
        mixed = jnp.dot(diff.astype(BF16), pw_ref[g], preferred_element_type=F32) * ps_ref[:, cols]
        acc = acc + jnp.dot(mixed.astype(BF16),
                            wout_ref[ATTN_WIDTH + g * LANES:ATTN_WIDTH + (g + 1) * LANES, :],
                            preferred_element_type=F32)
    out_ref[...] = x_ref[...] + acc


def _mix0(a_out, p_in, x2, w_out, pool_w, pool_scale, seq, tm):
    T, D = x2.shape
    pw = p_in.shape[1]
    hb = tm // POOL_HALO
    last = T // POOL_HALO - 1
    row = lambda w: pl.BlockSpec((tm, w), lambda i: (i, 0))
    full = lambda s: pl.BlockSpec(s, lambda i: (0,) * len(s))
    return pl.pallas_call(
        functools.partial(_mix0_kernel, seq=seq, tm=tm),
        grid=(T // tm,),
        in_specs=[
            row(ATTN_WIDTH), row(pw),
            pl.BlockSpec((POOL_HALO, pw), lambda i: (jnp.maximum(i * hb - 1, 0), 0)),
            pl.BlockSpec((POOL_HALO, pw), lambda i: (jnp.minimum((i + 1) * hb, last), 0)),
            row(D), full(w_out.shape), full(pool_w.shape), full((1, pw))],
        out_specs=row(D),
        out_shape=jax.ShapeDtypeStruct((T, D), F32),
        scratch_shapes=[pltpu.VMEM((tm + 2 * POOL_HALO, pw), F32)],
        compiler_params=_params(("parallel",)),
    )(a_out, p_in, p_in, p_in, x2, w_out.astype(BF16), pool_w.astype(BF16),
      pool_scale.reshape(1, pw))


def _sgu_kernel(x_ref, g_ref, win_ref, cn_ref, ws_ref, bs_ref, wout_ref, out_ref, gated_ref, *, tm):
    x = x_ref[...]
    width = wout_ref.shape[0]
    h = _rms(x, g_ref[...]).astype(BF16)
    u = _gelu(jnp.dot(h, win_ref[:, :width], preferred_element_type=F32))
    v = _gelu(jnp.dot(h, win_ref[:, width:], preferred_element_type=F32))
    vb = _rms(v, cn_ref[...]).astype(BF16)
    for n in range(tm // SGU_CHUNK):
        rows = slice(n * SGU_CHUNK, (n + 1) * SGU_CHUNK)
        for g in range(SGU_GROUPS):
            cols = slice(g * LANES, (g + 1) * LANES)
            mixed = jnp.dot(ws_ref[g], vb[rows, cols], preferred_element_type=F32) + bs_ref[:, cols]
            gated_ref[rows, cols] = (u[rows, cols] * mixed).astype(BF16)
    out_ref[...] = x + jnp.dot(gated_ref[...], wout_ref[...], preferred_element_type=F32)


def _sgu(x2, g, w_in, c_norm, w_s, b_s, w_out, tm):
    T, D = x2.shape
    width = w_out.shape[0]
    bias = jnp.repeat(b_s.T, width // SGU_GROUPS, axis=1)
    full = lambda s: pl.BlockSpec(s, lambda i: (0,) * len(s))
    return pl.pallas_call(
        functools.partial(_sgu_kernel, tm=tm),
        grid=(T // tm,),
        in_specs=[pl.BlockSpec((tm, D), lambda i: (i, 0)), full((1, D)), full(w_in.shape),
                  full((1, width)), full(w_s.shape), full(bias.shape), full(w_out.shape)],
        out_specs=pl.BlockSpec((tm, D), lambda i: (i, 0)),
        out_shape=jax.ShapeDtypeStruct((T, D), F32),
        scratch_shapes=[pltpu.VMEM((tm, width), BF16)],
        compiler_params=_params(("parallel",)),
    )(x2, g.reshape(1, D), w_in.astype(BF16), c_norm.reshape(1, width), w_s.astype(BF16), bias,
      w_out.astype(BF16))


def _cmpx(v, i, l, descending):
    hi, lo = jnp.maximum(v[i], v[l]), jnp.minimum(v[i], v[l])
    v[i], v[l] = (hi, lo) if descending else (lo, hi)


def _bitonic_sort_desc(v):
    v = list(v)
    n = len(v)
    k = 2
    while k <= n:
        j = k // 2
        while j >= 1:
            for i in range(n):
                l = i ^ j
                if l > i:
                    _cmpx(v, i, l, (i & k) == 0)
            j //= 2
        k *= 2
    return v


def _merge_top(a, b):
    n = len(a)
    v = [jnp.maximum(a[i], b[n - 1 - i]) for i in range(n)]
    j = n // 2
    while j >= 1:
        for i in range(n):
            if (i & j) == 0:
                _cmpx(v, i, i + j, True)
        j //= 2
    return v


def _top_sorted(vals, k):
    groups = [_bitonic_sort_desc(vals[i:i + k]) for i in range(0, len(vals), k)]
    while len(groups) > 1:
        groups = [_merge_top(groups[i], groups[i + 1]) for i in range(0, len(groups), 2)]
    return groups[0]


def _count_reached(s, t):
    assert len(t) == 16
    c16 = s >= t[15]
    c8 = s >= t[7]
    c4 = s >= jnp.where(c8, t[11], t[3])
    c2 = s >= jnp.where(c8, jnp.where(c4, t[13], t[9]), jnp.where(c4, t[5], t[1]))
    quarter = [jnp.where(c2, t[4 * q + 2], t[4 * q]) for q in range(4)]
    c1 = s >= jnp.where(c8, jnp.where(c4, quarter[3], quarter[2]), jnp.where(c4, quarter[1], quarter[0]))
    return (jnp.where(c8, 8.0, 0.0) + jnp.where(c4, 4.0, 0.0) + jnp.where(c2, 2.0, 0.0)
            + jnp.where(c1, 1.0, 0.0) + jnp.where(c16, 1.0, 0.0))


def _count(preds):
    tot = jnp.where(preds[0], 1.0, 0.0)
    for p in preds[1:]:
        tot = tot + jnp.where(p, 1.0, 0.0)
    return tot


def _twin_bf16(v):
    bits = pltpu.bitcast(v.astype(BF16).astype(F32), U32)
    return bits | (bits >> 16)


def _peer_gate_kernel(x_ref, g_ref, wq_ref, keys_ref, ht_ref, r2_ref, e2_ref, ne_ref,
                      km0_ref, km1_ref, om_ref, *, tg, heads, n_work):
    s = pl.program_id(0)

    @pl.when((s % heads == 0) & (s < n_work))
    def _():
        ht_ref[...] = _as_words(_rms(x_ref[...], g_ref[...]).T)

    @pl.when(s == 0)
    def _():
        km1_ref[...] = jnp.zeros_like(km1_ref)

    args = (wq_ref, keys_ref, ht_ref, r2_ref, e2_ref, ne_ref, om_ref, tg)
    pl.when(s % 2 == 0)(lambda: _gate_step(km0_ref, km1_ref, *args))
    pl.when(s % 2 == 1)(lambda: _gate_step(km1_ref, km0_ref, *args))


def _gate_step(km_new, km_old, wq_ref, keys_ref, ht_ref, r2_ref, e2_ref, ne_ref, om_ref, tg):
    K = PEER_TOPK
    nk = PEER_N_KEYS
    chunks = tg // LANES
    half_dim = wq_ref.shape[0] // 2

    def score():
        qt = jnp.dot(wq_ref[...], _as_bf16(ht_ref[...]), preferred_element_type=F32)
        for side in range(2):
            sc = jnp.dot(keys_ref[side], qt[side * half_dim:(side + 1) * half_dim].astype(BF16),
                         preferred_element_type=F32)
            for c in range(chunks):
                km_new[side, pl.ds(c, nk, stride=chunks), :] = sc[:, c * LANES:(c + 1) * LANES]

    def keyrows(k):
        return slice(k * chunks, (k + 1) * chunks)

    s1 = [km_old[0, keyrows(k), :] for k in range(nk)]
    s2 = [km_old[1, keyrows(k), :] for k in range(nk)]
    A = _top_sorted(s1, K)
    B = _top_sorted(s2, K)
    score()

    reach = [K // (i + 1) for i in range(K)]
    cand = [[A[i] + B[j] for j in range(reach[i])] for i in range(K)]
    ninf = jnp.full_like(A[0], -jnp.inf)
    rest = [c for row in cand[1:] for c in row]
    rest = rest + [ninf] * (-len(rest) % K)
    top = cand[0]
    for i in range(0, len(rest), K):
        top = _merge_top(top, _bitonic_sort_desc(rest[i:i + K]))
    thr = top[K - 1]

    gt = [_count([c > thr for c in row]) for row in cand]
    eq = [_count([c == thr for c in row]) for row in cand]
    total_gt = functools.reduce(lambda a, b: a + b, gt)
    room = float(K) - total_gt
    n = []
    for i in range(K):
        n.append(gt[i] + jnp.clip(room, 0.0, eq[i]))
        room = room - eq[i]

    e1 = [jnp.exp(a - A[0]) for a in A]
    e2 = [jnp.exp(b - B[0]) for b in B]
    z = jnp.zeros_like(A[0])
    for i in range(K):
        inner = jnp.zeros_like(z)
        for j in range(reach[i]):
            inner = inner + jnp.where(n[i] > float(j), e2[j], 0.0)
        z = z + e1[i] * inner
    inv_z = 1.0 / z

    pinf = jnp.full_like(A[0], jnp.inf)
    t = []
    for m in range(1, K + 1):
        tm_ = pinf
        for i in range(K):
            tm_ = jnp.minimum(tm_, jnp.where(n[i] >= float(m), A[i], pinf))
        t.append(tm_)

    b_up = B[::-1]
    for k in range(nk):
        om_ref[0, keyrows(k), :] = _count_reached(s1[k], t)
        om_ref[1, keyrows(k), :] = jnp.exp(s1[k] - A[0]) * inv_z
        om_ref[2, keyrows(k), :] = float(K) - _count_reached(s2[k], b_up)
        om_ref[3, keyrows(k), :] = jnp.exp(s2[k] - B[0])

    for c in range(chunks):
        cols = slice(c * LANES, (c + 1) * LANES)
        ne_ref[0, c] = _twin_bf16(om_ref[0, pl.ds(c, nk, stride=chunks), :])
        ne_ref[1, c] = _twin_bf16(om_ref[1, pl.ds(c, nk, stride=chunks), :])
        r2_ref[:, cols] = _as_words(om_ref[2, pl.ds(c, nk, stride=chunks), :])
        e2_ref[:, cols] = _as_words(om_ref[3, pl.ds(c, nk, stride=chunks), :])


def _peer_gates(x2, g, w_q, sub_keys):
    T, D = x2.shape
    tg = VREG_TOKENS
    heads, _, nk, half_dim = sub_keys.shape
    wq_t = w_q.T.astype(BF16)
    n_work = (T // tg) * heads

    def item(lag):
        def split(s):
            w = jnp.clip(s - lag, 0, n_work - 1)
            return w // heads, w % heads
        return split

    scored, chosen = item(0), item(1)
    return pl.pallas_call(
        functools.partial(_peer_gate_kernel, tg=tg, heads=heads, n_work=n_work),
        grid=(n_work + 1,),
        in_specs=[pl.BlockSpec((tg, D), lambda s: (scored(s)[0], 0)),
                  pl.BlockSpec((1, D), lambda s: (0, 0)),
                  pl.BlockSpec((2 * half_dim, D), lambda s: (scored(s)[1], 0)),
                  pl.BlockSpec((None, 2, nk, half_dim), lambda s: (scored(s)[1], 0, 0, 0))],
        out_specs=[pl.BlockSpec((D // 2, tg), lambda s: (0, scored(s)[0])),
                   pl.BlockSpec((nk // 2, tg), lambda s: (chosen(s)[1], chosen(s)[0])),
                   pl.BlockSpec((nk // 2, tg), lambda s: (chosen(s)[1], chosen(s)[0])),
                   pl.BlockSpec((None, 2, tg // LANES, nk, LANES),
                                lambda s: (chosen(s)[1], 0, chosen(s)[0], 0, 0))],
        out_shape=[jax.ShapeDtypeStruct((D // 2, T), U32),
                   jax.ShapeDtypeStruct((heads * nk // 2, T), U32),
                   jax.ShapeDtypeStruct((heads * nk // 2, T), U32),
                   jax.ShapeDtypeStruct((heads, 2, T // LANES, nk, LANES), U32)],
        scratch_shapes=[pltpu.VMEM((2, nk * SUBLANES, LANES), F32),
                        pltpu.VMEM((2, nk * SUBLANES, LANES), F32),
                        pltpu.VMEM((4, nk * SUBLANES, LANES), F32)],
        compiler_params=_params(("arbitrary",)),
    )(x2, g.reshape(1, D), wq_t, sub_keys.astype(BF16))


KEY_BLOCK = 4
PIECE_ROWS = 256


def _gate_hidden(hid_ref, ga_ref, r2_ref, e2_ref, ne_ref, first_key, c, al0):
    nk = PEER_N_KEYS
    tiles = range(0, nk, BF16_ROWS)
    cols = slice(c * LANES, (c + 1) * LANES)

    def row_tile(h, q, al):
        return _as_bf16(ne_ref[h, q, c, pl.ds(first_key + al, SUBLANES, stride=0), :])

    gates = [[None for _ in tiles] for _ in range(KEY_BLOCK)]
    for h in range(PEER_HEADS):
        counts = [row_tile(h, 0, al0 + k) for k in range(KEY_BLOCK)]
        factors = [row_tile(h, 1, al0 + k) for k in range(KEY_BLOCK)]
        for ti, r0 in enumerate(tiles):
            hrows = slice((h * nk + r0) // 2, (h * nk + r0) // 2 + SUBLANES)
            rank, e2 = _as_bf16(r2_ref[hrows, cols]), _as_bf16(e2_ref[hrows, cols])
            for k in range(KEY_BLOCK):
                term = jnp.where(rank < counts[k], e2, jnp.zeros_like(e2)) * factors[k]
                gates[k][ti] = term if gates[k][ti] is None else gates[k][ti] + term
    for k in range(KEY_BLOCK):
        for ti, r0 in enumerate(tiles):
            rows = slice((al0 + k) * nk + r0, (al0 + k) * nk + r0 + BF16_ROWS)
            ga_ref[rows, cols] = gates[k][ti] * _gelu(hid_ref[rows, cols].astype(BF16))


def _peer_expert_kernel(ht_ref, wd_ref, wu_ref, r2_ref, e2_ref, ne_ref, x_ref, gf_ref, out_ref,
                        acc_ref, hid0_ref, hid1_ref, ga0_ref, ga1_ref, *, final_norm, n_blocks, n_work):
    s = pl.program_id(0)
    eb = hid0_ref.shape[0]
    gate_item = jnp.clip(s - 1, 0, n_work - 1)
    up_block = jnp.clip(s - 2, 0, n_work - 1) % n_blocks
    first_key = (gate_item % n_blocks) * (eb // PEER_N_KEYS)

    @pl.when(s == 0)
    def _():
        hid1_ref[...] = jnp.zeros_like(hid1_ref)
        ga0_ref[...] = jnp.zeros_like(ga0_ref)

    @pl.when(up_block == 0)
    def _():
        acc_ref[...] = jnp.zeros_like(acc_ref)

    def stages(hid_new, hid_old, ga_new, ga_old):
        tm = hid_new.shape[1]
        d_model = acc_ref.shape[0]

        def up(n, m):
            cols = slice(n * MXU_WIDTH, (n + 1) * MXU_WIDTH)
            rows = slice(m * PIECE_ROWS, (m + 1) * PIECE_ROWS)
            words = slice(m * PIECE_ROWS // 2, (m + 1) * PIECE_ROWS // 2)
            acc_ref[rows, cols] += jnp.dot(_as_bf16(wu_ref[words, :]), ga_old[:, cols],
                                           preferred_element_type=F32)

        def down(n, m):
            cols = slice(n * MXU_WIDTH, (n + 1) * MXU_WIDTH)
            rows = slice(m * PIECE_ROWS, (m + 1) * PIECE_ROWS)
            words = slice(m * PIECE_ROWS // 2, (m + 1) * PIECE_ROWS // 2)
            hid_new[rows, cols] = jnp.dot(_as_bf16(wd_ref[words, :]), _as_bf16(ht_ref[:, cols]),
                                          preferred_element_type=F32)

        ups, downs = d_model // PIECE_ROWS, eb // PIECE_ROWS
        assert downs % ups == 0
        mxu = [functools.partial(f, n, m) for n in range(tm // MXU_WIDTH) for u in range(ups)
               for f, m in [(up, u)] + [(down, u * (downs // ups) + d) for d in range(downs // ups)]]
        vpu = [functools.partial(_gate_hidden, hid_old, ga_new, r2_ref, e2_ref, ne_ref, first_key, c, al0)
               for al0 in range(0, eb // PEER_N_KEYS, KEY_BLOCK) for c in range(tm // LANES)]
        done = 0
        for i, piece in enumerate(mxu):
            piece()
            upto = -(-(i + 1) * len(vpu) // len(mxu))
            for g in vpu[done:upto]:
                g()
            done = upto

    pl.when(s % 2 == 0)(lambda: stages(hid0_ref, hid1_ref, ga1_ref, ga0_ref))
    pl.when(s % 2 == 1)(lambda: stages(hid1_ref, hid0_ref, ga0_ref, ga1_ref))

    @pl.when((s >= 2) & (up_block == n_blocks - 1))
    def _():
        y = x_ref[...] + acc_ref[...].T
        out_ref[...] = _rms(y, gf_ref[...]) if final_norm else y


def _peer_experts(ht, r2, e2, ne, x2, wd_words, wu_words, layer, g_final, final_norm, tm, eb):
    T, D = x2.shape
    n_exp = wu_words.shape[2]
    heads, _, _, nk, _ = ne.shape
    n_blocks = n_exp // eb
    n_work = (T // tm) * n_blocks

    def item(lag):
        def split(s):
            w = jnp.clip(s - lag, 0, n_work - 1)
            return w // n_blocks, w % n_blocks
        return split

    down, gate, up = item(0), item(1), item(2)
    return pl.pallas_call(
        functools.partial(_peer_expert_kernel, final_norm=final_norm, n_blocks=n_blocks, n_work=n_work),
        grid=(n_work + 2,),
        in_specs=[pl.BlockSpec((D // 2, tm), lambda s: (0, down(s)[0])),
                  pl.BlockSpec((None, eb // 2, D), lambda s: (layer, down(s)[1], 0)),
                  pl.BlockSpec((None, D // 2, eb), lambda s: (layer, 0, up(s)[1])),
                  pl.BlockSpec((heads * nk // 2, tm), lambda s: (0, gate(s)[0])),
                  pl.BlockSpec((heads * nk // 2, tm), lambda s: (0, gate(s)[0])),
                  pl.BlockSpec((heads, 2, tm // LANES, nk, LANES), lambda s: (0, 0, gate(s)[0], 0, 0)),
                  pl.BlockSpec((tm, D), lambda s: (up(s)[0], 0)),
                  pl.BlockSpec((1, D), lambda s: (0, 0))],
        out_specs=pl.BlockSpec((tm, D), lambda s: (up(s)[0], 0)),
        out_shape=jax.ShapeDtypeStruct((T, D), F32),
        scratch_shapes=[pltpu.VMEM((D, tm), F32), pltpu.VMEM((eb, tm), F32), pltpu.VMEM((eb, tm), F32),
                        pltpu.VMEM((eb, tm), BF16), pltpu.VMEM((eb, tm), BF16)],
        compiler_params=_params(("arbitrary",)),
    )(ht, wd_words, wu_words, r2, e2, ne, x2, g_final.reshape(1, D))


def kernel(x, norm_mix, norm_ffn, norm_final, ab_w_in, ab_w_out, pool_w, pool_scale, c_w_in,
           c_norm, c_w_s, c_b_s, c_w_out, peer_w_q, peer_sub_keys, peer_w_down, peer_w_up):
    B, S, D = x.shape
    depth = norm_mix.shape[0]
    tm = 512
    eb = 2048
    assert S % tm == 0 and (B * S) % VREG_TOKENS == 0
    assert all(S % (d * QUERY_BLOCK) == 0 for d in DILATIONS)
    x2 = x.reshape(B * S, D)
    wd_words = _pack_rows(peer_w_down, transpose=False)
    wu_words = _pack_rows(peer_w_up, transpose=True)
    for layer in range(depth):
        j = layer // 2
        if layer % 2 == 0:
            qkv, p_in = _inproj(x2, norm_mix[layer], ab_w_in[j], S, tm)
            a_out = _attention(qkv, B, S)
            x2 = _mix0(a_out, p_in, x2, ab_w_out[j], pool_w[j], pool_scale[j], S, tm)
        else:
            x2 = _sgu(x2, norm_mix[layer], c_w_in[j], c_norm[j], c_w_s[j], c_b_s[j], c_w_out[j], tm)
        ht, r2, e2, ne = _peer_gates(x2, norm_ffn[layer], peer_w_q[layer], peer_sub_keys[layer])
        x2 = _peer_experts(ht, r2, e2, ne, x2, wd_words, wu_words, layer, norm_final,
                           layer == depth - 1, tm, eb)
    return x2.reshape(B, S, D)
```

```python
import functools
import math

import jax
import jax.numpy as jnp
from jax import lax
from jax.experimental import pallas as pl
from jax.experimental.pallas import tpu as pltpu

F32 = jnp.float32
BF16 = jnp.bfloat16

LANES = 128
SUBLANES = 8
BF16_ROWS = 2 * SUBLANES
MXU_WIDTH = 256
VREG_TOKENS = SUBLANES * LANES
VMEM_LIMIT = 56 * 1024 * 1024

ATTN_HEADS = 8
HEAD_DIM = 64
ATTN_WIDTH = ATTN_HEADS * HEAD_DIM
DILATIONS = (1, 4, 16)
ATTN_HALF = 64
QUERY_BLOCK = 128
BLOCK_UNROLL = 8
ROPE_THETA = 500000.0
ROPE_DIM = HEAD_DIM // 4
POOL_WINDOWS = (2, 4, 8, 16)
POOL_HALO = 8
SGU_CHUNK = 128
SGU_GROUPS = 8
PEER_HEADS = 8
PEER_N_KEYS = 128
PEER_TOPK = 16
RMS_EPS = 1e-6
NEG_INF = -1e30
GELU_C = math.sqrt(2.0 / math.pi)
GELU_CUBIC = 0.044715


def _params(semantics):
    return pltpu.CompilerParams(dimension_semantics=semantics, vmem_limit_bytes=VMEM_LIMIT)


def _rms(x, g):
    return x * lax.rsqrt(jnp.mean(x * x, axis=-1, keepdims=True) + RMS_EPS) * g


U32 = jnp.uint32


def _as_bf16(words):
    return pltpu.bitcast(words, BF16)


def _as_words(x):
    return pltpu.bitcast(x.astype(BF16), U32)


def _pack_kernel(w_ref, o_ref, *, transpose):
    w = w_ref[...]
    o_ref[...] = _as_words(w.T if transpose else w)


def _pack_rows(w, transpose, rows=512):
    N, R, C = w.shape
    out_block, out_map = ((None, C // 2, rows), lambda n, i: (n, 0, i)) if transpose else (
        (None, rows // 2, C), lambda n, i: (n, i, 0))
    out_shape = (N, C // 2, R) if transpose else (N, R // 2, C)
    return pl.pallas_call(
        functools.partial(_pack_kernel, transpose=transpose),
        grid=(N, R // rows),
        in_specs=[pl.BlockSpec((None, rows, C), lambda n, i: (n, i, 0))],
        out_specs=pl.BlockSpec(out_block, out_map),
        out_shape=jax.ShapeDtypeStruct(out_shape, U32),
        compiler_params=_params(("parallel", "parallel")),
    )(w)


def _gelu(x):
    half = 0.5 * x
    return half + half * jnp.tanh(x * (GELU_C + (GELU_C * GELU_CUBIC) * (x * x)))


def _inproj_kernel(x_ref, g_ref, w_ref, cos_ref, sa_ref, sb_ref, qkv_ref, p_ref):
    h = _rms(x_ref[...], g_ref[...]).astype(BF16)
    proj = jnp.dot(h, w_ref[...], preferred_element_type=F32)
    c, sa, sb = cos_ref[...], sa_ref[...], sb_ref[...]
    half = ROPE_DIM // 2
    for slab in range(2 * ATTN_WIDTH // LANES):
        t = proj[:, slab * LANES:(slab + 1) * LANES]
        up = pltpu.roll(t, LANES - half, axis=1)
        dn = pltpu.roll(t, half, axis=1)
        rotated = t * c + up * sa + dn * sb
        qkv_ref[slab] = rotated * HEAD_DIM ** -0.5 if slab < ATTN_WIDTH // LANES else rotated
    for slab in range(2 * ATTN_WIDTH // LANES, 3 * ATTN_WIDTH // LANES):
        qkv_ref[slab] = proj[:, slab * LANES:(slab + 1) * LANES]
    p_ref[...] = proj[:, 3 * ATTN_WIDTH:]


def _rope_tables(seq):
    half = ROPE_DIM // 2
    pos = jnp.arange(seq, dtype=F32)
    inv_freq = ROPE_THETA ** (-jnp.arange(0, ROPE_DIM, 2, dtype=F32) / ROPE_DIM)
    ang = pos[:, None] * inv_freq[None, :]
    cos, sin = jnp.cos(ang), jnp.sin(ang)
    pad = HEAD_DIM - ROPE_DIM
    ones = jnp.ones((seq, pad), F32)
    zeros = jnp.zeros((seq, pad), F32)
    zh = jnp.zeros((seq, half), F32)
    c = jnp.concatenate([cos, cos, ones], axis=1)
    sa = jnp.concatenate([-sin, zh, zeros], axis=1)
    sb = jnp.concatenate([zh, sin, zeros], axis=1)
    return tuple(jnp.concatenate([t, t], axis=1) for t in (c, sa, sb))


def _inproj(x2, g, w_in, seq, tm):
    T, D = x2.shape
    n_out = w_in.shape[1]
    pool_w = n_out - 3 * ATTN_WIDTH
    c, sa, sb = _rope_tables(seq)
    spt = seq // tm
    tab = pl.BlockSpec((tm, LANES), lambda i: (i % spt, 0))
    return pl.pallas_call(
        _inproj_kernel,
        grid=(T // tm,),
        in_specs=[pl.BlockSpec((tm, D), lambda i: (i, 0)),
                  pl.BlockSpec((1, D), lambda i: (0, 0)),
                  pl.BlockSpec((D, n_out), lambda i: (0, 0)),
                  tab, tab, tab],
        out_specs=[pl.BlockSpec((3 * ATTN_WIDTH // LANES, tm, LANES), lambda i: (0, i, 0)),
                   pl.BlockSpec((tm, pool_w), lambda i: (i, 0))],
        out_shape=[jax.ShapeDtypeStruct((3 * ATTN_WIDTH // LANES, T, LANES), F32),
                   jax.ShapeDtypeStruct((T, pool_w), F32)],
        compiler_params=_params(("parallel",)),
    )(x2, g.reshape(1, D), w_in.astype(BF16), c, sa, sb)


def _attn_kernel(q_ref, k_ref, v_ref, a_ref, o_scr, lse_scr, *, seq):
    lane = lax.broadcasted_iota(jnp.int32, (QUERY_BLOCK, LANES), 1)
    first_head = lane < HEAD_DIM
    head_masks = (first_head, jnp.logical_not(first_head))

    for pat, dil in enumerate(DILATIONS):
        length = seq // dil
        kw = min(QUERY_BLOCK + 2 * ATTN_HALF, length)
        blocks = length // QUERY_BLOCK
        ones = jnp.ones((kw, LANES), BF16)

        def group(g, carry, dil=dil, length=length, kw=kw, blocks=blocks, pat=pat, ones=ones):
            loaded = []
            for j in range(BLOCK_UNROLL):
                it = g * BLOCK_UNROLL + j
                res, qb = it // blocks, it % blocks
                q0 = qb * QUERY_BLOCK
                start = jnp.clip(q0 - ATTN_HALF, 0, length - kw)

                def rows(ref, first, count, res=res):
                    return ref[pl.ds(res + dil * first, count, stride=dil), :]

                qpos = q0 + lax.broadcasted_iota(jnp.int32, (QUERY_BLOCK, kw), 0)
                kpos = start + lax.broadcasted_iota(jnp.int32, (QUERY_BLOCK, kw), 1)
                loaded.append((rows(q_ref, q0, QUERY_BLOCK).astype(BF16), rows(k_ref, start, kw).astype(BF16),
                               rows(v_ref, start, kw).astype(BF16), jnp.abs(kpos - qpos) <= ATTN_HALF,
                               pl.ds(res + dil * q0, QUERY_BLOCK, stride=dil)))
            chains = [(j, mask) for j in range(BLOCK_UNROLL) for mask in head_masks]
            scores = []
            for j, mask in chains:
                qp, kp, _, valid, _ = loaded[j]
                qm = jnp.where(mask, qp, jnp.zeros_like(qp))
                s = lax.dot_general(qm, kp, (((1,), (1,)), ((), ())), preferred_element_type=F32)
                scores.append(jnp.where(valid, s, NEG_INF))
            tops = [jnp.max(s, axis=-1, keepdims=True) for s in scores]
            probs = [jnp.exp(s - m).astype(BF16) for s, m in zip(scores, tops)]
            totals = [jnp.dot(p, ones, preferred_element_type=F32) for p in probs]
            mixes = [jnp.dot(p, loaded[j][2], preferred_element_type=F32) for p, (j, _) in zip(probs, chains)]
            outs = [o / t for o, t in zip(mixes, totals)]
            lses = [m + jnp.log(t) for m, t in zip(tops, totals)]
            for j in range(BLOCK_UNROLL):
                dst = loaded[j][4]
                o_scr[pat, dst, :] = jnp.where(first_head, outs[2 * j], outs[2 * j + 1])
                lse_scr[pat, dst, :] = jnp.where(first_head, lses[2 * j], lses[2 * j + 1])
            return carry

        assert (dil * blocks) % BLOCK_UNROLL == 0
        lax.fori_loop(0, dil * blocks // BLOCK_UNROLL, group, 0)

    la, lb, lc = lse_scr[0], lse_scr[1], lse_scr[2]
    m = jnp.maximum(jnp.maximum(la, lb), lc)
    ea, eb, ec = jnp.exp(la - m), jnp.exp(lb - m), jnp.exp(lc - m)
    a_ref[...] = ((o_scr[0] * ea + o_scr[1] * eb + o_scr[2] * ec) / (ea + eb + ec)).astype(BF16)


def _attention(qkv, batch, seq):
    pairs = ATTN_WIDTH // LANES

    def part(j):
        return pl.BlockSpec((None, seq, LANES), lambda b, hp: (j * pairs + hp, b, 0))

    return pl.pallas_call(
        functools.partial(_attn_kernel, seq=seq),
        grid=(batch, pairs),
        in_specs=[part(0), part(1), part(2)],
        out_specs=pl.BlockSpec((seq, LANES), lambda b, hp: (b, hp)),
        out_shape=jax.ShapeDtypeStruct((batch * seq, ATTN_WIDTH), BF16),
        scratch_shapes=[pltpu.VMEM((len(DILATIONS), seq, LANES), F32),
                        pltpu.VMEM((len(DILATIONS), seq, LANES), F32)],
        compiler_params=_params(("parallel", "parallel")),
    )(qkv, qkv, qkv)


def _mix0_kernel(a_ref, pc_ref, pp_ref, pn_ref, x_ref, wout_ref, pw_ref, ps_ref,
                 out_ref, ext_ref, *, seq, tm):
    ti = pl.program_id(0) % (seq // tm)
    acc = jnp.dot(a_ref[...], wout_ref[:ATTN_WIDTH, :], preferred_element_type=F32)

    ext_ref[:POOL_HALO, :] = jnp.where(ti > 0, pp_ref[...], 0.0)
    ext_ref[POOL_HALO:POOL_HALO + tm, :] = pc_ref[...]
    ext_ref[POOL_HALO + tm:, :] = jnp.where(ti < seq // tm - 1, pn_ref[...], 0.0)
    pos = ti * tm + lax.broadcasted_iota(jnp.int32, (tm, 1), 0)
    for g, win in enumerate(POOL_WINDOWS):
        cols = slice(g * LANES, (g + 1) * LANES)
        before, after = win // 2, win - win // 2
        tot = ext_ref[POOL_HALO - before:POOL_HALO - before + tm, cols]
        for d in range(-before + 1, after):
            tot = tot + ext_ref[POOL_HALO + d:POOL_HALO + d + tm, cols]
        cnt = (jnp.minimum(pos + after, seq) - jnp.maximum(pos - before, 0)).astype(F32)
        diff = tot / cnt - pc_ref[:, cols]
        mixed = jnp.dot(diff.astype(BF16), pw_ref[g], preferred_element_type=F32) * ps_ref[:, cols]
        acc = acc + jnp.dot(mixed.astype(BF16),
                            wout_ref[ATTN_WIDTH + g * LANES:ATTN_WIDTH + (g + 1) * LANES, :],
                            preferred_element_type=F32)
    out_ref[...] = x_ref[...] + acc


def _mix0(a_out, p_in, x2, w_out, pool_w, pool_scale, seq, tm):
    T, D = x2.shape
    pw = p_in.shape[1]
    hb = tm // POOL_HALO
    last = T // POOL_HALO - 1
    row = lambda w: pl.BlockSpec((tm, w), lambda i: (i, 0))
    full = lambda s: pl.BlockSpec(s, lambda i: (0,) * len(s))
    return pl.pallas_call(
        functools.partial(_mix0_kernel, seq=seq, tm=tm),
        grid=(T // tm,),
        in_specs=[
            row(ATTN_WIDTH), row(pw),
            pl.BlockSpec((POOL_HALO, pw), lambda i: (jnp.maximum(i * hb - 1, 0), 0)),
            pl.BlockSpec((POOL_HALO, pw), lambda i: (jnp.minimum((i + 1) * hb, last), 0)),
            row(D), full(w_out.shape), full(pool_w.shape), full((1, pw))],
        out_specs=row(D),
        out_shape=jax.ShapeDtypeStruct((T, D), F32),
        scratch_shapes=[pltpu.VMEM((tm + 2 * POOL_HALO, pw), F32)],
        compiler_params=_params(("parallel",)),
    )(a_out, p_in, p_in, p_in, x2, w_out.astype(BF16), pool_w.astype(BF16),
      pool_scale.reshape(1, pw))


def _sgu_kernel(x_ref, g_ref, win_ref, cn_ref, ws_ref, bs_ref, wout_ref, out_ref, gated_ref, *, tm):
    x = x_ref[...]
    width = wout_ref.shape[0]
    h = _rms(x, g_ref[...]).astype(BF16)
    u = _gelu(jnp.dot(h, win_ref[:, :width], preferred_element_type=F32))
    v = _gelu(jnp.dot(h, win_ref[:, width:], preferred_element_type=F32))
    vb = _rms(v, cn_ref[...]).astype(BF16)
    for n in range(tm // SGU_CHUNK):
        rows = slice(n * SGU_CHUNK, (n + 1) * SGU_CHUNK)
        for g in range(SGU_GROUPS):
            cols = slice(g * LANES, (g + 1) * LANES)
            mixed = jnp.dot(ws_ref[g], vb[rows, cols], preferred_element_type=F32) + bs_ref[:, cols]
            gated_ref[rows, cols] = (u[rows, cols] * mixed).astype(BF16)
    out_ref[...] = x + jnp.dot(gated_ref[...], wout_ref[...], preferred_element_type=F32)


def _sgu(x2, g, w_in, c_norm, w_s, b_s, w_out, tm):
    T, D = x2.shape
    width = w_out.shape[0]
    bias = jnp.repeat(b_s.T, width // SGU_GROUPS, axis=1)
    full = lambda s: pl.BlockSpec(s, lambda i: (0,) * len(s))
    return pl.pallas_call(
        functools.partial(_sgu_kernel, tm=tm),
        grid=(T // tm,),
        in_specs=[pl.BlockSpec((tm, D), lambda i: (i, 0)), full((1, D)), full(w_in.shape),
                  full((1, width)), full(w_s.shape), full(bias.shape), full(w_out.shape)],
        out_specs=pl.BlockSpec((tm, D), lambda i: (i, 0)),
        out_shape=jax.ShapeDtypeStruct((T, D), F32),
        scratch_shapes=[pltpu.VMEM((tm, width), BF16)],
        compiler_params=_params(("parallel",)),
    )(x2, g.reshape(1, D), w_in.astype(BF16), c_norm.reshape(1, width), w_s.astype(BF16), bias,
      w_out.astype(BF16))


def _cmpx(v, i, l, descending):
    hi, lo = jnp.maximum(v[i], v[l]), jnp.minimum(v[i], v[l])
    v[i], v[l] = (hi, lo) if descending else (lo, hi)


def _bitonic_sort_desc(v):
    v = list(v)
    n = len(v)
    k = 2
    while k <= n:
        j = k // 2
        while j >= 1:
            for i in range(n):
                l = i ^ j
                if l > i:
                    _cmpx(v, i, l, (i & k) == 0)
            j //= 2
        k *= 2
    return v


def _merge_top(a, b):
    n = len(a)
    v = [jnp.maximum(a[i], b[n - 1 - i]) for i in range(n)]
    j = n // 2
    while j >= 1:
        for i in range(n):
            if (i & j) == 0:
                _cmpx(v, i, i + j, True)
        j //= 2
    return v


def _top_sorted(vals, k):
    groups = [_bitonic_sort_desc(vals[i:i + k]) for i in range(0, len(vals), k)]
    while len(groups) > 1:
        groups = [_merge_top(groups[i], groups[i + 1]) for i in range(0, len(groups), 2)]
    return groups[0]


def _count_reached(s, t):
    assert len(t) == 16
    c16 = s >= t[15]
    c8 = s >= t[7]
    c4 = s >= jnp.where(c8, t[11], t[3])
    c2 = s >= jnp.where(c8, jnp.where(c4, t[13], t[9]), jnp.where(c4, t[5], t[1]))
    quarter = [jnp.where(c2, t[4 * q + 2], t[4 * q]) for q in range(4)]
    c1 = s >= jnp.where(c8, jnp.where(c4, quarter[3], quarter[2]), jnp.where(c4, quarter[1], quarter[0]))
    return (jnp.where(c8, 8.0, 0.0) + jnp.where(c4, 4.0, 0.0) + jnp.where(c2, 2.0, 0.0)
            + jnp.where(c1, 1.0, 0.0) + jnp.where(c16, 1.0, 0.0))


def _count(preds):
    tot = jnp.where(preds[0], 1.0, 0.0)
    for p in preds[1:]:
        tot = tot + jnp.where(p, 1.0, 0.0)
    return tot


def _twin_bf16(v):
    bits = pltpu.bitcast(v.astype(BF16).astype(F32), U32)
    return bits | (bits >> 16)


def _peer_gate_kernel(x_ref, g_ref, wq_ref, keys_ref, ht_ref, r2_ref, e2_ref, ne_ref,
                      km0_ref, km1_ref, om_ref, *, tg, heads, n_work):
    s = pl.program_id(0)

    @pl.when((s % heads == 0) & (s < n_work))
    def _():
        ht_ref[...] = _as_words(_rms(x_ref[...], g_ref[...]).T)

    @pl.when(s == 0)
    def _():
        km1_ref[...] = jnp.zeros_like(km1_ref)

    args = (wq_ref, keys_ref, ht_ref, r2_ref, e2_ref, ne_ref, om_ref, tg)
    pl.when(s % 2 == 0)(lambda: _gate_step(km0_ref, km1_ref, *args))
    pl.when(s % 2 == 1)(lambda: _gate_step(km1_ref, km0_ref, *args))


def _gate_step(km_new, km_old, wq_ref, keys_ref, ht_ref, r2_ref, e2_ref, ne_ref, om_ref, tg):
    K = PEER_TOPK
    nk = PEER_N_KEYS
    chunks = tg // LANES
    half_dim = wq_ref.shape[0] // 2

    def score():
        qt = jnp.dot(wq_ref[...], _as_bf16(ht_ref[...]), preferred_element_type=F32)
        for side in range(2):
            sc = jnp.dot(keys_ref[side], qt[side * half_dim:(side + 1) * half_dim].astype(BF16),
                         preferred_element_type=F32)
            for c in range(chunks):
                km_new[side, pl.ds(c, nk, stride=chunks), :] = sc[:, c * LANES:(c + 1) * LANES]

    def keyrows(k):
        return slice(k * chunks, (k + 1) * chunks)

    s1 = [km_old[0, keyrows(k), :] for k in range(nk)]
    s2 = [km_old[1, keyrows(k), :] for k in range(nk)]
    A = _top_sorted(s1, K)
    B = _top_sorted(s2, K)
    score()

    reach = [K // (i + 1) for i in range(K)]
    cand = [[A[i] + B[j] for j in range(reach[i])] for i in range(K)]
    ninf = jnp.full_like(A[0], -jnp.inf)
    rest = [c for row in cand[1:] for c in row]
    rest = rest + [ninf] * (-len(rest) % K)
    top = cand[0]
    for i in range(0, len(rest), K):
        top = _merge_top(top, _bitonic_sort_desc(rest[i:i + K]))
    thr = top[K - 1]

    gt = [_count([c > thr for c in row]) for row in cand]
    eq = [_count([c == thr for c in row]) for row in cand]
    total_gt = functools.reduce(lambda a, b: a + b, gt)
    room = float(K) - total_gt
    n = []
    for i in range(K):
        n.append(gt[i] + jnp.clip(room, 0.0, eq[i]))
        room = room - eq[i]

    e1 = [jnp.exp(a - A[0]) for a in A]
    e2 = [jnp.exp(b - B[0]) for b in B]
    z = jnp.zeros_like(A[0])
    for i in range(K):
        inner = jnp.zeros_like(z)
        for j in range(reach[i]):
            inner = inner + jnp.where(n[i] > float(j), e2[j], 0.0)
        z = z + e1[i] * inner
    inv_z = 1.0 / z

    pinf = jnp.full_like(A[0], jnp.inf)
    t = []
    for m in range(1, K + 1):
        tm_ = pinf
        for i in range(K):
            tm_ = jnp.minimum(tm_, jnp.where(n[i] >= float(m), A[i], pinf))
        t.append(tm_)

    b_up = B[::-1]
    for k in range(nk):
        om_ref[0, keyrows(k), :] = _count_reached(s1[k], t)
        om_ref[1, keyrows(k), :] = jnp.exp(s1[k] - A[0]) * inv_z
        om_ref[2, keyrows(k), :] = float(K) - _count_reached(s2[k], b_up)
        om_ref[3, keyrows(k), :] = jnp.exp(s2[k] - B[0])

    for c in range(chunks):
        cols = slice(c * LANES, (c + 1) * LANES)
        ne_ref[0, c] = _twin_bf16(om_ref[0, pl.ds(c, nk, stride=chunks), :])
        ne_ref[1, c] = _twin_bf16(om_ref[1, pl.ds(c, nk, stride=chunks), :])
        r2_ref[:, cols] = _as_words(om_ref[2, pl.ds(c, nk, stride=chunks), :])
        e2_ref[:, cols] = _as_words(om_ref[3, pl.ds(c, nk, stride=chunks), :])


def _peer_gates(x2, g, w_q, sub_keys):
    T, D = x2.shape
    tg = VREG_TOKENS
    heads, _, nk, half_dim = sub_keys.shape
    wq_t = w_q.T.astype(BF16)
    n_work = (T // tg) * heads

    def item(lag):
        def split(s):
            w = jnp.clip(s - lag, 0, n_work - 1)
            return w // heads, w % heads
        return split

    scored, chosen = item(0), item(1)
    return pl.pallas_call(
        functools.partial(_peer_gate_kernel, tg=tg, heads=heads, n_work=n_work),
        grid=(n_work + 1,),
        in_specs=[pl.BlockSpec((tg, D), lambda s: (scored(s)[0], 0)),
                  pl.BlockSpec((1, D), lambda s: (0, 0)),
                  pl.BlockSpec((2 * half_dim, D), lambda s: (scored(s)[1], 0)),
                  pl.BlockSpec((None, 2, nk, half_dim), lambda s: (scored(s)[1], 0, 0, 0))],
        out_specs=[pl.BlockSpec((D // 2, tg), lambda s: (0, scored(s)[0])),
                   pl.BlockSpec((nk // 2, tg), lambda s: (chosen(s)[1], chosen(s)[0])),
                   pl.BlockSpec((nk // 2, tg), lambda s: (chosen(s)[1], chosen(s)[0])),
                   pl.BlockSpec((None, 2, tg // LANES, nk, LANES),
                                lambda s: (chosen(s)[1], 0, chosen(s)[0], 0, 0))],
        out_shape=[jax.ShapeDtypeStruct((D // 2, T), U32),
                   jax.ShapeDtypeStruct((heads * nk // 2, T), U32),
                   jax.ShapeDtypeStruct((heads * nk // 2, T), U32),
                   jax.ShapeDtypeStruct((heads, 2, T // LANES, nk, LANES), U32)],
        scratch_shapes=[pltpu.VMEM((2, nk * SUBLANES, LANES), F32),
                        pltpu.VMEM((2, nk * SUBLANES, LANES), F32),
                        pltpu.VMEM((4, nk * SUBLANES, LANES), F32)],
        compiler_params=_params(("arbitrary",)),
    )(x2, g.reshape(1, D), wq_t, sub_keys.astype(BF16))


KEY_BLOCK = 4
PIECE_ROWS = 256


def _gate_hidden(hid_ref, ga_ref, r2_ref, e2_ref, ne_ref, first_key, c, al0):
    nk = PEER_N_KEYS
    tiles = range(0, nk, BF16_ROWS)
    cols = slice(c * LANES, (c + 1) * LANES)

    def row_tile(h, q, al):
        return _as_bf16(ne_ref[h, q, c, pl.ds(first_key + al, SUBLANES, stride=0), :])

    gates = [[None for _ in tiles] for _ in range(KEY_BLOCK)]
    for h in range(PEER_HEADS):
        counts = [row_tile(h, 0, al0 + k) for k in range(KEY_BLOCK)]
        factors = [row_tile(h, 1, al0 + k) for k in range(KEY_BLOCK)]
        for ti, r0 in enumerate(tiles):
            hrows = slice((h * nk + r0) // 2, (h * nk + r0) // 2 + SUBLANES)
            rank, e2 = _as_bf16(r2_ref[hrows, cols]), _as_bf16(e2_ref[hrows, cols])
            for k in range(KEY_BLOCK):
                term = jnp.where(rank < counts[k], e2, jnp.zeros_like(e2)) * factors[k]
                gates[k][ti] = term if gates[k][ti] is None else gates[k][ti] + term
    for k in range(KEY_BLOCK):
        for ti, r0 in enumerate(tiles):
            rows = slice((al0 + k) * nk + r0, (al0 + k) * nk + r0 + BF16_ROWS)
            ga_ref[rows, cols] = gates[k][ti] * _gelu(hid_ref[rows, cols].astype(BF16))


def _peer_expert_kernel(ht_ref, wd_ref, wu_ref, r2_ref, e2_ref, ne_ref, x_ref, gf_ref, out_ref,
                        acc_ref, hid_ref, ga0_ref, ga1_ref, *, final_norm, n_blocks, n_work):
    s = pl.program_id(0)
    eb = hid_ref.shape[0]
    gate_item = jnp.clip(s, 0, n_work - 1)
    up_block = jnp.clip(s - 1, 0, n_work - 1) % n_blocks
    first_key = (gate_item % n_blocks) * (eb // PEER_N_KEYS)

    @pl.when(s == 0)
    def _():
        ga1_ref[...] = jnp.zeros_like(ga1_ref)

    @pl.when(up_block == 0)
    def _():
        acc_ref[...] = jnp.zeros_like(acc_ref)

    def stages(ga_new, ga_old):
        hid_new = hid_old = hid_ref
        tm = hid_new.shape[1]
        d_model = acc_ref.shape[0]

        def up(n, m):
            cols = slice(n * MXU_WIDTH, (n + 1) * MXU_WIDTH)
            rows = slice(m * PIECE_ROWS, (m + 1) * PIECE_ROWS)
            words = slice(m * PIECE_ROWS // 2, (m + 1) * PIECE_ROWS // 2)
            acc_ref[rows, cols] += jnp.dot(_as_bf16(wu_ref[words, :]), ga_old[:, cols],
                                           preferred_element_type=F32)

        def down(n, m):
            cols = slice(n * MXU_WIDTH, (n + 1) * MXU_WIDTH)
            rows = slice(m * PIECE_ROWS, (m + 1) * PIECE_ROWS)
            words = slice(m * PIECE_ROWS // 2, (m + 1) * PIECE_ROWS // 2)
            hid_new[rows, cols] = jnp.dot(_as_bf16(wd_ref[words, :]), _as_bf16(ht_ref[:, cols]),
                                          preferred_element_type=F32)

        ups, downs = d_model // PIECE_ROWS, eb // PIECE_ROWS
        assert downs % ups == 0
        per_group = downs // ups
        keys_per_group = per_group * PIECE_ROWS // PEER_N_KEYS
        assert keys_per_group == KEY_BLOCK
        for n in range(tm // MXU_WIDTH):
            for u in range(ups):
                up(n, u)
                for d in range(per_group):
                    down(n, u * per_group + d)
                for c in range(n * (MXU_WIDTH // LANES), (n + 1) * (MXU_WIDTH // LANES)):
                    _gate_hidden(hid_old, ga_new, r2_ref, e2_ref, ne_ref, first_key, c, u * KEY_BLOCK)

    pl.when(s % 2 == 0)(lambda: stages(ga0_ref, ga1_ref))
    pl.when(s % 2 == 1)(lambda: stages(ga1_ref, ga0_ref))

    @pl.when((s >= 1) & (up_block == n_blocks - 1))
    def _():
        y = x_ref[...] + acc_ref[...].T
        out_ref[...] = _rms(y, gf_ref[...]) if final_norm else y


def _peer_experts(ht, r2, e2, ne, x2, wd_words, wu_words, layer, g_final, final_norm, tm, eb):
    T, D = x2.shape
    n_exp = wu_words.shape[2]
    heads, _, _, nk, _ = ne.shape
    n_blocks = n_exp // eb
    n_work = (T // tm) * n_blocks

    def item(lag):
        def split(s):
            w = jnp.clip(s - lag, 0, n_work - 1)
            return w // n_blocks, w % n_blocks
        return split

    down, gate, up = item(0), item(0), item(1)
    return pl.pallas_call(
        functools.partial(_peer_expert_kernel, final_norm=final_norm, n_blocks=n_blocks, n_work=n_work),
        grid=(n_work + 1,),
        in_specs=[pl.BlockSpec((D // 2, tm), lambda s: (0, down(s)[0])),
                  pl.BlockSpec((None, eb // 2, D), lambda s: (layer, down(s)[1], 0)),
                  pl.BlockSpec((None, D // 2, eb), lambda s: (layer, 0, up(s)[1])),
                  pl.BlockSpec((heads * nk // 2, tm), lambda s: (0, gate(s)[0])),
                  pl.BlockSpec((heads * nk // 2, tm), lambda s: (0, gate(s)[0])),
                  pl.BlockSpec((heads, 2, tm // LANES, nk, LANES), lambda s: (0, 0, gate(s)[0], 0, 0)),
                  pl.BlockSpec((tm, D), lambda s: (up(s)[0], 0)),
                  pl.BlockSpec((1, D), lambda s: (0, 0))],
        out_specs=pl.BlockSpec((tm, D), lambda s: (up(s)[0], 0)),
        out_shape=jax.ShapeDtypeStruct((T, D), F32),
        scratch_shapes=[pltpu.VMEM((D, tm), F32), pltpu.VMEM((eb, tm), F32),
                        pltpu.VMEM((eb, tm), BF16), pltpu.VMEM((eb, tm), BF16)],
        compiler_params=_params(("arbitrary",)),
    )(ht, wd_words, wu_words, r2, e2, ne, x2, g_final.reshape(1, D))


def kernel(x, norm_mix, norm_ffn, norm_final, ab_w_in, ab_w_out, pool_w, pool_scale, c_w_in,
           c_norm, c_w_s, c_b_s, c_w_out, peer_w_q, peer_sub_keys, peer_w_down, peer_w_up):
    B, S, D = x.shape
    depth = norm_mix.shape[0]
    tm = 512
    eb = 2048
    assert S % tm == 0 and (B * S) % VREG_TOKENS == 0
    assert all(S % (d * QUERY_BLOCK) == 0 for d in DILATIONS)
    x2 = x.reshape(B * S, D)
    wd_words = _pack_rows(peer_w_down, transpose=False)
    wu_words = _pack_rows(peer_w_up, transpose=True)
    for layer in range(depth):
        j = layer // 2
        if layer % 2 == 0:
            qkv, p_in = _inproj(x2, norm_mix[layer], ab_w_in[j], S, tm)
            a_out = _attention(qkv, B, S)
            x2 = _mix0(a_out, p_in, x2, ab_w_out[j], pool_w[j], pool_scale[j], S, tm)
        else:
            x2 = _sgu(x2, norm_mix[layer], c_w_in[j], c_norm[j], c_w_s[j], c_b_s[j], c_w_out[j], tm)
        ht, r2, e2, ne = _peer_gates(x2, norm_ffn[layer], peer_w_q[layer], peer_sub_keys[layer])
        x2 = _peer_experts(ht, r2, e2, ne, x2, wd_words, wu_words, layer, norm_final,
                           layer == depth - 1, tm, eb)
    return x2.reshape(B, S, D)
```
